```python
import math
import jax, jax.numpy as jnp
from jax import lax
import numpy as np

D_MODEL = 2048
BATCH = 1
SEQ = 16384
DEPTH = 2

FOX_HEADS = 4
FOX_HEAD_DIM = 128
FOX_WIDTH = FOX_HEADS * FOX_HEAD_DIM
Q_BLOCK = 128
GDN_HEADS = 4
GDN_HEAD_DIM = 128
GDN_WIDTH = GDN_HEADS * GDN_HEAD_DIM
GDN_CONV = 4
GDN_CHUNK = 64
S5_GROUPS = 32
S5_GROUP_DIM = 16
S5_WIDTH = S5_GROUPS * S5_GROUP_DIM
S5_STATE = 64
GLA_HEADS = 4
GLA_HEAD_K = 64
GLA_HEAD_V = 128
GLA_K_WIDTH = GLA_HEADS * GLA_HEAD_K
GLA_V_WIDTH = GLA_HEADS * GLA_HEAD_V
GLA_RANK = 16
GLA_GATE_NORMALIZER = 16.0
GLA_CHUNK = 64
N_BRANCHES = 4
BRANCH_WIDTH = 512
D_FF = 5504
FFN_CONV = 3
NORM_EPS = 1e-6

IN_SIZES = (
    FOX_WIDTH, FOX_WIDTH, FOX_WIDTH, FOX_HEADS,
    GDN_WIDTH, GDN_WIDTH, GDN_WIDTH, GDN_WIDTH, GDN_HEADS, GDN_HEADS,
    S5_WIDTH,
    GLA_K_WIDTH, GLA_K_WIDTH, GLA_V_WIDTH, GLA_V_WIDTH, GLA_RANK,
    N_BRANCHES * D_MODEL,
)
N_IN = sum(IN_SIZES)

kernel_name = "hybrid_fox_gdn_s5_gla_block"


def rmsnorm(x, gain):
    xf = x.astype(jnp.float32)
    y = xf * lax.rsqrt(jnp.mean(xf * xf, axis=-1, keepdims=True) + NORM_EPS)
    return (y * gain.astype(jnp.float32)).astype(x.dtype)


def l2norm(x):
    return x * lax.rsqrt(jnp.sum(x * x, axis=-1, keepdims=True) + NORM_EPS)


def causal_dwconv(x, w):
    k = w.shape[0]
    return lax.conv_general_dilated(
        x, w[:, None, :].astype(x.dtype), window_strides=(1,), padding=[(k - 1, 0)],
        dimension_numbers=("NWC", "WIO", "NWC"), feature_group_count=x.shape[-1])


def fox_attention(q, k, v, log_f):
    B, L, H, Dh = q.shape
    nb = L // Q_BLOCK
    c = jnp.cumsum(log_f, axis=1)
    c_k = jnp.transpose(c, (0, 2, 1))[:, :, None, :]
    qb = jnp.moveaxis((q * Dh ** -0.5).reshape(B, nb, Q_BLOCK, H, Dh), 1, 0)
    cb = jnp.moveaxis(c.reshape(B, nb, Q_BLOCK, H), 1, 0)
    pos_k = jnp.arange(L)

    def block(args):
        i, q_i, c_i = args
        pos_q = i * Q_BLOCK + jnp.arange(Q_BLOCK)
        s = jnp.einsum("bqhd,bkhd->bhqk", q_i, k)
        s = s + (jnp.transpose(c_i, (0, 2, 1))[..., None] - c_k)
        s = jnp.where(pos_k[None, :] <= pos_q[:, None], s, -jnp.inf)
        p = jax.nn.softmax(s, axis=-1)
        return jnp.einsum("bhqk,bkhd->bqhd", p, v)

    o = lax.map(block, (jnp.arange(nb), qb, cb))
    return jnp.moveaxis(o, 0, 1).reshape(B, L, H, Dh)


def gated_delta_rule(q, k, v, g, beta):
    B, L, H, Dk = q.shape
    Dv = v.shape[-1]
    C = GDN_CHUNK
    n = L // C

    def chunks(t):
        return t.reshape(B, n, C, H, -1).transpose(0, 3, 1, 2, 4)

    q = chunks(l2norm(q) * Dk ** -0.5)
    k = chunks(l2norm(k))
    v = chunks(v)
    g = g.reshape(B, n, C, H).transpose(0, 3, 1, 2)
    beta = beta.reshape(B, n, C, H).transpose(0, 3, 1, 2)
    gc = jnp.cumsum(g, axis=-1)
    idx = jnp.arange(C)
    strict = idx[:, None] > idx[None, :]
    causal = idx[:, None] >= idx[None, :]
    decay = jnp.exp(jnp.minimum(gc[..., :, None] - gc[..., None, :], 0.0))
    a_kk = jnp.where(strict, beta[..., :, None] * jnp.einsum("bhncd,bhnsd->bhncs", k, k) * decay, 0.0)
    tri = a_kk + jnp.eye(C, dtype=jnp.float32)
    gam = jnp.exp(gc)
    rhs = jnp.concatenate([(beta * gam)[..., None] * k, beta[..., None] * v], axis=-1)
    sol = lax.linalg.triangular_solve(tri, rhs, left_side=True, lower=True, unit_diagonal=True)
    w_mat, u_mat = sol[..., :Dk], sol[..., Dk:]
    a_qk = jnp.where(causal, jnp.einsum("bhncd,bhnsd->bhncs", q, k) * decay, 0.0)
    q_dec = q * gam[..., None]
    k_dec = k * jnp.exp(gc[..., -1:] - gc)[..., None]
    g_last = jnp.exp(gc[..., -1])

    def step(state, inp):
        w_c, u_c, aqk_c, qd_c, kd_c, gl_c = inp
        delta = u_c - jnp.einsum("bhcd,bhde->bhce", w_c, state)
        out = jnp.einsum("bhcd,bhde->bhce", qd_c, state) + jnp.einsum("bhcs,bhse->bhce", aqk_c, delta)
        state = gl_c[..., None, None] * state + jnp.einsum("bhcd,bhce->bhde", kd_c, delta)
        return state, out

    xs = tuple(jnp.moveaxis(t, 2, 0) for t in (w_mat, u_mat, a_qk, q_dec, k_dec, g_last))
    s0 = jnp.zeros((B, H, Dk, Dv), jnp.float32)
    _, o = lax.scan(step, s0, xs)
    return o.transpose(1, 0, 3, 2, 4).reshape(B, L, H, Dv)


def gla_attention(q, k, v, log_gk):
    B, L, H, Dk = q.shape
    C = GLA_CHUNK
    n = L // C

    def chunks(t):
        return t.reshape(B, n, C, H, -1).transpose(1, 0, 3, 2, 4)

    qc = chunks(q * Dk ** -0.5)
    kc = chunks(k)
    vc = chunks(v)
    gc = jnp.cumsum(chunks(log_gk), axis=3)
    idx = jnp.arange(C)
    causal = idx[:, None] >= idx[None, :]

    def step(state, inp):
        q_c, k_c, v_c, g_c = inp
        diff = g_c[:, :, :, None, :] - g_c[:, :, None, :, :]
        dec = jnp.where(causal[:, :, None], jnp.exp(jnp.minimum(diff, 0.0)), 0.0)
        a = jnp.einsum("bhcd,bhsd,bhcsd->bhcs", q_c, k_c, dec)
        out = jnp.einsum("bhcd,bhde->bhce", q_c * jnp.exp(g_c), state) + jnp.einsum("bhcs,bhse->bhce", a, v_c)
        g_last = g_c[:, :, -1]
        k_dec = k_c * jnp.exp(g_last[:, :, None, :] - g_c)
        state = jnp.exp(g_last)[..., None] * state + jnp.einsum("bhcd,bhce->bhde", k_dec, v_c)
        return state, out

    s0 = jnp.zeros((B, H, Dk, v.shape[-1]), jnp.float32)
    _, o = lax.scan(step, s0, (qc, kc, vc, gc))
    return o.transpose(1, 0, 3, 2, 4).reshape(B, L, H, -1)


def s5_ssm(u, a_re, a_im, b_re, b_im, c_re, c_im, d, log_dt):
    f32 = jnp.float32
    lam = lax.complex(a_re.astype(f32), a_im.astype(f32))
    dt = jnp.exp(log_dt.astype(f32))[:, None]
    lam_bar = jnp.exp(lam * dt)
    b = lax.complex(b_re.astype(f32), b_im.astype(f32))
    b_bar = ((lam_bar - 1.0) / lam)[..., None] * b
    bu = jnp.einsum("gph,blgh->blgp", b_bar, u.astype(jnp.complex64))
    a_seq = jnp.broadcast_to(lam_bar, bu.shape)

    def combine(e1, e2):
        a1, x1 = e1
        a2, x2 = e2
        return a1 * a2, a2 * x1 + x2

    _, states = lax.associative_scan(combine, (a_seq, bu), axis=1)
    c = lax.complex(c_re.astype(f32), c_im.astype(f32))
    y = jnp.einsum("ghp,blgp->blgh", c, states).real
    return y + d.astype(f32).reshape(S5_GROUPS, S5_GROUP_DIM) * u


def hybrid_mixer(h, w_in, fox_f_bias, gdn_conv, gdn_a_log, gdn_dt_bias, gdn_norm,
                 s5_a_re, s5_a_im, s5_b_re, s5_b_im, s5_c_re, s5_c_im, s5_d, s5_log_dt, s5_w_glu,
                 gla_w_gk, gla_b_gk, gla_norm, w_branch, w_out):
    f32 = jnp.float32
    B, L, _ = h.shape
    z = h @ w_in
    (fq, fk, fv, ff, bq, bk, bv, bz, ba, bb, su,
     dq, dk, dv, dg, dr, gate_logits) = jnp.split(z, np.cumsum(IN_SIZES)[:-1], axis=-1)

    def heads(t, nh):
        return t.reshape(B, L, nh, -1)

    log_f = jax.nn.log_sigmoid((ff + fox_f_bias).astype(f32))
    y_a = fox_attention(heads(fq, FOX_HEADS).astype(f32), heads(fk, FOX_HEADS).astype(f32),
                        heads(fv, FOX_HEADS).astype(f32), log_f).reshape(B, L, FOX_WIDTH)

    qkv = jax.nn.silu(causal_dwconv(jnp.concatenate([bq, bk, bv], axis=-1), gdn_conv)).astype(f32)
    q_b, k_b, v_b = jnp.split(qkv, 3, axis=-1)
    g_b = -jnp.exp(gdn_a_log.astype(f32)) * jax.nn.softplus((ba + gdn_dt_bias).astype(f32))
    beta_b = jax.nn.sigmoid(bb.astype(f32))
    o_b = gated_delta_rule(heads(q_b, GDN_HEADS), heads(k_b, GDN_HEADS), heads(v_b, GDN_HEADS), g_b, beta_b)
    y_b = (rmsnorm(o_b, gdn_norm) * jax.nn.silu(heads(bz, GDN_HEADS).astype(f32))).reshape(B, L, GDN_WIDTH)

    u_c = su.astype(f32).reshape(B, L, S5_GROUPS, S5_GROUP_DIM)
    y_c = jax.nn.gelu(s5_ssm(u_c, s5_a_re, s5_a_im, s5_b_re, s5_b_im, s5_c_re, s5_c_im, s5_d, s5_log_dt)
                      .reshape(B, L, S5_WIDTH))
    y_c = y_c * jax.nn.sigmoid(y_c @ s5_w_glu.astype(f32))

    log_gk = jax.nn.log_sigmoid((dr @ gla_w_gk + gla_b_gk).astype(f32)) / GLA_GATE_NORMALIZER
    o_d = gla_attention(heads(dq, GLA_HEADS).astype(f32), heads(dk, GLA_HEADS).astype(f32),
                        heads(dv, GLA_HEADS).astype(f32), heads(log_gk, GLA_HEADS))
    y_d = (rmsnorm(o_d, gla_norm) * jax.nn.silu(heads(dg, GLA_HEADS).astype(f32))).reshape(B, L, GLA_V_WIDTH)

    gates = jax.nn.sigmoid(gate_logits.reshape(B, L, N_BRANCHES, D_MODEL))
    branches = (y_a, y_b, y_c, y_d)
    merged = gates[:, :, 0] * (branches[0].astype(h.dtype) @ w_branch[0])
    for i in range(1, N_BRANCHES):
        merged = merged + gates[:, :, i] * (branches[i].astype(h.dtype) @ w_branch[i])
    return merged @ w_out


def conv_ffn(h, w_up, conv_w, w_down):
    up = causal_dwconv(h @ w_up, conv_w)
    gate, val = jnp.split(up, 2, axis=-1)
    return (jax.nn.silu(gate) * val) @ w_down


def setup_inputs(seed: int = 0) -> dict:
    key = jax.random.key(seed)
    ks = jax.random.split(key, 32)
    f32 = jnp.float32

    def nrm(k, shape, scale):
        return scale * jax.random.normal(k, shape, f32)

    def gain(k, n):
        return 1.0 + nrm(k, (DEPTH, n), 0.02)

    G, P, Hg = S5_GROUPS, S5_STATE, S5_GROUP_DIM
    dt = jnp.exp(jax.random.uniform(ks[9], (DEPTH, GDN_HEADS), f32, math.log(1e-3), math.log(1e-1)))
    return {
        "x": nrm(ks[0], (BATCH, SEQ, D_MODEL), 1.0),
        "norm_mix_pre": gain(ks[1], D_MODEL),
        "norm_mix_post": gain(ks[2], D_MODEL),
        "norm_ffn_pre": gain(ks[3], D_MODEL),
        "norm_ffn_post": gain(ks[4], D_MODEL),
        "w_in": nrm(ks[5], (DEPTH, D_MODEL, N_IN), D_MODEL ** -0.5),
        "fox_f_bias": jax.random.uniform(ks[6], (DEPTH, FOX_HEADS), f32, 1.0, 6.0),
        "gdn_conv": nrm(ks[7], (DEPTH, GDN_CONV, 3 * GDN_WIDTH), GDN_CONV ** -0.5),
        "gdn_a_log": jnp.log(jax.random.uniform(ks[8], (DEPTH, GDN_HEADS), f32, 1.0, 16.0)),
        "gdn_dt_bias": dt + jnp.log(-jnp.expm1(-dt)),
        "gdn_norm": gain(ks[10], GDN_HEAD_DIM),
        "s5_a_re": -0.5 + nrm(ks[11], (DEPTH, G, P), 0.01),
        "s5_a_im": jnp.pi * jnp.arange(P, dtype=f32) + nrm(ks[12], (DEPTH, G, P), 0.01),
        "s5_b_re": nrm(ks[13], (DEPTH, G, P, Hg), (2.0 * Hg) ** -0.5),
        "s5_b_im": nrm(ks[14], (DEPTH, G, P, Hg), (2.0 * Hg) ** -0.5),
        "s5_c_re": nrm(ks[15], (DEPTH, G, Hg, P), (2.0 * P) ** -0.5),
        "s5_c_im": nrm(ks[16], (DEPTH, G, Hg, P), (2.0 * P) ** -0.5),
        "s5_d": nrm(ks[17], (DEPTH, S5_WIDTH), 1.0),
        "s5_log_dt": jax.random.uniform(ks[18], (DEPTH, G), f32, math.log(1e-3), math.log(1e-1)),
        "s5_w_glu": nrm(ks[19], (DEPTH, S5_WIDTH, S5_WIDTH), S5_WIDTH ** -0.5),
        "gla_w_gk": nrm(ks[20], (DEPTH, GLA_RANK, GLA_K_WIDTH), GLA_RANK ** -0.5),
        "gla_b_gk": nrm(ks[21], (DEPTH, GLA_K_WIDTH), 0.1),
        "gla_norm": gain(ks[22], GLA_HEAD_V),
        "w_branch": nrm(ks[23], (DEPTH, N_BRANCHES, BRANCH_WIDTH, D_MODEL), BRANCH_WIDTH ** -0.5),
        "w_out": nrm(ks[24], (DEPTH, D_MODEL, D_MODEL), D_MODEL ** -0.5),
        "ffn_w_up": nrm(ks[25], (DEPTH, D_MODEL, 2 * D_FF), D_MODEL ** -0.5),
        "ffn_conv": nrm(ks[26], (DEPTH, FFN_CONV, 2 * D_FF), FFN_CONV ** -0.5),
        "ffn_w_down": nrm(ks[27], (DEPTH, D_FF, D_MODEL), D_FF ** -0.5),
    }


def reference(x, norm_mix_pre, norm_mix_post, norm_ffn_pre, norm_ffn_post, w_in, fox_f_bias,
              gdn_conv, gdn_a_log, gdn_dt_bias, gdn_norm, s5_a_re, s5_a_im, s5_b_re, s5_b_im,
              s5_c_re, s5_c_im, s5_d, s5_log_dt, s5_w_glu, gla_w_gk, gla_b_gk, gla_norm,
              w_branch, w_out, ffn_w_up, ffn_conv, ffn_w_down):
    for l in range(DEPTH):
        h = rmsnorm(x, norm_mix_pre[l])
        m = hybrid_mixer(h, w_in[l], fox_f_bias[l], gdn_conv[l], gdn_a_log[l], gdn_dt_bias[l], gdn_norm[l],
                         s5_a_re[l], s5_a_im[l], s5_b_re[l], s5_b_im[l], s5_c_re[l], s5_c_im[l],
                         s5_d[l], s5_log_dt[l], s5_w_glu[l], gla_w_gk[l], gla_b_gk[l], gla_norm[l],
                         w_branch[l], w_out[l])
        x = x + rmsnorm(m, norm_mix_post[l])
        h = rmsnorm(x, norm_ffn_pre[l])
        x = x + rmsnorm(conv_ffn(h, ffn_w_up[l], ffn_conv[l], ffn_w_down[l]), norm_ffn_post[l])
    return x
```

```python
import functools
import math

import jax
import jax.numpy as jnp
from jax import lax
from jax.experimental import pallas as pl
from jax.experimental.pallas import tpu as pltpu

F32 = jnp.float32
BF16 = jnp.bfloat16
HIGHEST = lax.Precision.HIGHEST

D_MODEL = 2048
N_HEADS = 4
HEAD_DIM = 128
BRANCH_WIDTH = 512
GDN_CONV = 4
CHUNK = 64
GLA_HEAD_K = 64
GLA_K_WIDTH = 256
GLA_RANK = 16
GLA_GATE_NORMALIZER = 16.0
GLA_SUB = 16
S5_GROUPS = 32
S5_GROUP_DIM = 16
S5_STATE = 64
S5_T = 16
D_FF = 5504
D_FF_PAD = 5632
FFN_CONV = 3
N_BRANCHES = 4
NORM_EPS = 1e-6

LANES = 128
SUBLANES = 8
VMEM_LIMIT_BYTES = 56 * 1024 * 1024

MIX_WIDTH = 5632
COL_FOX = 0
COL_GDN_QKV = 1536
COL_GDN_Z = 3072
COL_S5 = 3584
COL_GLA_Q = 4096
COL_GLA_K = 4352
COL_GLA_V = 4608
COL_GLA_G = 5120
SM_DR = 0
SM_FF = 16
SM_BA = 20
SM_BB = 24
SM_ROWS = 32


def _tiles(seq):
    return dict(
        tm_in=min(512, seq), tn_in=512,
        t_att=min(1024, seq),
        r_rec=min(128, seq),
        tm_merge=min(512, seq),
        tm_out=min(512, seq),
        tm_ffn=min(512, seq), tf_ffn=512,
        tm_glu=min(1024, seq),
        tn_scan=min(128, seq // S5_T),
    )


def _cparams(*sem):
    return pltpu.CompilerParams(dimension_semantics=sem, vmem_limit_bytes=VMEM_LIMIT_BYTES)


def _sigmoid(x):
    return 1.0 / (1.0 + jnp.exp(-x))


def _softplus(x):
    return jnp.maximum(x, 0.0) + jnp.log1p(jnp.exp(-jnp.abs(x)))


def _log_sigmoid(x):
    return jnp.minimum(x, 0.0) - jnp.log1p(jnp.exp(-jnp.abs(x)))


def _rms(x, gain):
    return x * lax.rsqrt(jnp.mean(x * x, axis=-1, keepdims=True) + NORM_EPS) * gain


def _dot(a, b):
    return jnp.dot(a, b, preferred_element_type=F32)


def _dot_nt(a, b):
    return lax.dot_general(a, b, (((1,), (1,)), ((), ())), preferred_element_type=F32)


def _dot_tn(a, b):
    return lax.dot_general(a, b, (((0,), (0,)), ((), ())), preferred_element_type=F32)


def _split_bf16(a):
    hi = a.astype(BF16)
    lo = (a - hi.astype(F32)).astype(BF16)
    return hi, lo


def _dot3(a_split, b_split):
    ah, al = a_split
    bh, bl = b_split
    return _dot(ah, bh) + (_dot(ah, bl) + _dot(al, bh))


def _iota(shape, axis):
    return lax.broadcasted_iota(jnp.int32, shape, axis)


def _in_proj_body(x_ref, g_ref, w_ref, ws_ref, z_ref, zs_ref, h_ref):
    @pl.when(pl.program_id(1) == 0)
    def _():
        h = _rms(x_ref[...], g_ref[...]).astype(BF16)
        h_ref[...] = h
        zs_ref[...] = _dot(h, ws_ref[...])

    z_ref[...] = _dot(h_ref[...], w_ref[...])


def _in_proj(x, gain, w_mix, w_small, t):
    seq = x.shape[0]
    tm, tn = t["tm_in"], t["tn_in"]
    return pl.pallas_call(
        _in_proj_body,
        grid=(seq // tm, MIX_WIDTH // tn),
        in_specs=[
            pl.BlockSpec((tm, D_MODEL), lambda i, j: (i, 0)),
            pl.BlockSpec((1, D_MODEL), lambda i, j: (0, 0)),
            pl.BlockSpec((D_MODEL, tn), lambda i, j: (0, j)),
            pl.BlockSpec((D_MODEL, LANES), lambda i, j: (0, 0)),
        ],
        out_specs=[
            pl.BlockSpec((tm, tn), lambda i, j: (i, j)),
            pl.BlockSpec((tm, LANES), lambda i, j: (i, 0)),
            pl.BlockSpec((tm, D_MODEL), lambda i, j: (i, 0)),
        ],
        out_shape=[
            jax.ShapeDtypeStruct((seq, MIX_WIDTH), F32),
            jax.ShapeDtypeStruct((seq, LANES), F32),
            jax.ShapeDtypeStruct((seq, D_MODEL), BF16),
        ],
        compiler_params=_cparams("arbitrary", "arbitrary"),
        name="in_proj",
    )(x, gain, w_mix, w_small)


def _cum_body(ff_ref, b_ref, c_ref, *, rows):
    lf = _log_sigmoid(ff_ref[...] + b_ref[...]).reshape(N_HEADS * rows, LANES)
    upper = (_iota((LANES, LANES), 0) <= _iota((LANES, LANES), 1)).astype(F32)
    loc = jnp.dot(lf, upper, precision=HIGHEST, preferred_element_type=F32)
    tot = jnp.broadcast_to(loc[:, LANES - 1:LANES], loc.shape)
    n = N_HEADS * rows
    rr, cc = _iota((n, n), 0), _iota((n, n), 1)
    shift = int(math.log2(rows))
    same_head = lax.shift_right_logical(rr, shift) == lax.shift_right_logical(cc, shift)
    low = ((cc < rr) & same_head).astype(F32)
    off = jnp.dot(low, tot, precision=HIGHEST, preferred_element_type=F32)
    c_ref[...] = (loc + off).reshape(N_HEADS, rows, LANES)


def _forget_cumsum(ff_t, bias):
    seq = ff_t.shape[1]
    rows = seq // LANES
    assert rows & (rows - 1) == 0 and rows % SUBLANES == 0
    c = pl.pallas_call(
        functools.partial(_cum_body, rows=rows),
        out_shape=jax.ShapeDtypeStruct((N_HEADS, rows, LANES), F32),
        compiler_params=pltpu.CompilerParams(vmem_limit_bytes=VMEM_LIMIT_BYTES),
        name="fox_cumsum",
    )(ff_t.reshape(N_HEADS, rows, LANES), bias.reshape(N_HEADS, 1, 1))
    return c.reshape(N_HEADS, seq)


def _fox_body(qi_ref, kj_ref, q_ref, k_ref, v_ref, cq_ref, ck_ref, o_ref,
              m_ref, l_ref, acc_ref, qs_ref):
    p = pl.program_id(1)
    qi, kj = qi_ref[p], kj_ref[p]

    @pl.when(kj == 0)
    def _():
        m_ref[...] = jnp.full(m_ref.shape, -jnp.inf, F32)
        l_ref[...] = jnp.zeros(l_ref.shape, F32)
        acc_ref[...] = jnp.zeros(acc_ref.shape, F32)
        qs_ref[...] = (q_ref[...] * (HEAD_DIM ** -0.5)).astype(BF16)

    def step(diagonal):
        s = _dot_nt(qs_ref[...], k_ref[...].astype(BF16))
        s = s + (cq_ref[0, :, 0:1] - ck_ref[0])
        if diagonal:
            s = jnp.where(_iota(s.shape, 1) <= _iota(s.shape, 0), s, -jnp.inf)
        m_prev = m_ref[...]
        m_new = jnp.maximum(m_prev, jnp.max(s, axis=-1, keepdims=True))
        alpha = jnp.exp(m_prev - m_new)
        e = jnp.exp(s - m_new)
        l_ref[...] = alpha * l_ref[...] + jnp.sum(e, axis=-1, keepdims=True)
        acc_ref[...] = alpha * acc_ref[...] + _dot(e.astype(BF16), v_ref[...].astype(BF16))
        m_ref[...] = m_new

    @pl.when(kj < qi)
    def _():
        step(False)

    @pl.when(kj == qi)
    def _():
        step(True)
        o_ref[...] = (acc_ref[...] / l_ref[...]).astype(o_ref.dtype)


def _fox_attention(z, c, t):
    seq = z.shape[0]
    tq = t["t_att"]
    nq = seq // tq
    pairs = [(i, j) for i in range(nq) for j in range(i + 1)]
    qi = jnp.asarray([a for a, _ in pairs], jnp.int32)
    kj = jnp.asarray([b for _, b in pairs], jnp.int32)
    cb = COL_FOX // HEAD_DIM
    grid_spec = pltpu.PrefetchScalarGridSpec(
        num_scalar_prefetch=2,
        grid=(N_HEADS, len(pairs)),
        in_specs=[
            pl.BlockSpec((tq, HEAD_DIM), lambda h, p, qi, kj: (qi[p], cb + h)),
            pl.BlockSpec((tq, HEAD_DIM), lambda h, p, qi, kj: (kj[p], cb + N_HEADS + h)),
            pl.BlockSpec((tq, HEAD_DIM), lambda h, p, qi, kj: (kj[p], cb + 2 * N_HEADS + h)),
            pl.BlockSpec((1, 1, tq), lambda h, p, qi, kj: (h, 0, qi[p])),
            pl.BlockSpec((1, 1, tq), lambda h, p, qi, kj: (h, 0, kj[p])),
        ],
        out_specs=pl.BlockSpec((tq, HEAD_DIM), lambda h, p, qi, kj: (qi[p], h)),
        scratch_shapes=[
            pltpu.VMEM((tq, 1), F32),
            pltpu.VMEM((tq, 1), F32),
            pltpu.VMEM((tq, HEAD_DIM), F32),
            pltpu.VMEM((tq, HEAD_DIM), BF16),
        ],
    )
    c3 = c.reshape(N_HEADS, 1, seq)
    return pl.pallas_call(
        _fox_body,
        grid_spec=grid_spec,
        out_shape=jax.ShapeDtypeStruct((seq, BRANCH_WIDTH), BF16),
        compiler_params=_cparams("arbitrary", "arbitrary"),
        name="fox_attention",
    )(qi, kj, z, z, z, c3, c3)


def _unit_lower_inverse(a):
    n = a.shape[0]
    eye = (_iota((n, n), 0) == _iota((n, n), 1)).astype(F32)
    p = eye - a
    m_split = _split_bf16(a)
    for _ in range(int(math.log2(CHUNK)) - 1):
        m = _dot3(m_split, m_split)
        m_split = _split_bf16(m)
        p = p + _dot3(_split_bf16(p), m_split)
    return p


def _gdn_body(qkv_ref, bz_ref, zs_ref, zst_ref, cw_ref, alr_ref, alc_ref, dtr_ref, dtc_ref,
              gn_ref, o_ref, ext_ref, s_ref, *, rows):
    @pl.when(pl.program_id(0) == 0)
    def _():
        ext_ref[0:SUBLANES, :] = jnp.zeros((SUBLANES, ext_ref.shape[1]), F32)
        s_ref[...] = jnp.zeros(s_ref.shape, F32)

    x = qkv_ref[...]
    ext_ref[SUBLANES:SUBLANES + rows, :] = x
    cw = cw_ref[...]
    conv = cw[GDN_CONV - 1:GDN_CONV] * x
    for j in range(GDN_CONV - 1):
        off = SUBLANES - (GDN_CONV - 1) + j
        conv = conv + cw[j:j + 1] * ext_ref[off:off + rows, :]
    ext_ref[0:SUBLANES, :] = x[rows - SUBLANES:rows, :]
    qkv = conv * _sigmoid(conv)

    zs = zs_ref[...]
    zst = zst_ref[...]
    g_col = -jnp.exp(alr_ref[...]) * _softplus(zs[:, SM_BA:SM_BA + N_HEADS] + dtr_ref[...])
    g_row = -jnp.exp(alc_ref[...]) * _softplus(zst[SM_BA:SM_BA + N_HEADS, :] + dtc_ref[...])
    beta = _sigmoid(zs[:, SM_BB:SM_BB + N_HEADS])
    rr, cc = _iota((rows, rows), 0), _iota((rows, rows), 1)
    shift = int(math.log2(CHUNK))
    same = lax.shift_right_logical(rr, shift) == lax.shift_right_logical(cc, shift)
    incl = (same & (cc <= rr)).astype(F32)
    gc_col = jnp.dot(incl, g_col, precision=HIGHEST, preferred_element_type=F32)
    gc_row = _dot_nt_highest(g_row, incl)
    strict = same & (cc < rr)
    n_chunks = rows // CHUNK
    rc, ccn = _iota((CHUNK, CHUNK), 0), _iota((CHUNK, CHUNK), 1)
    causal = ccn <= rc

    for h in range(N_HEADS):
        q = qkv[:, h * HEAD_DIM:(h + 1) * HEAD_DIM]
        k = qkv[:, BRANCH_WIDTH + h * HEAD_DIM:BRANCH_WIDTH + (h + 1) * HEAD_DIM]
        v = qkv[:, 2 * BRANCH_WIDTH + h * HEAD_DIM:2 * BRANCH_WIDTH + (h + 1) * HEAD_DIM]
        q = q * lax.rsqrt(jnp.sum(q * q, axis=-1, keepdims=True) + NORM_EPS) * (HEAD_DIM ** -0.5)
        k = k * lax.rsqrt(jnp.sum(k * k, axis=-1, keepdims=True) + NORM_EPS)
        gcc = gc_col[:, h:h + 1]
        gcr = gc_row[h:h + 1, :]
        bt = beta[:, h:h + 1]
        kb = k.astype(BF16)
        decay = jnp.exp(jnp.minimum(gcc - gcr, 0.0))
        a_kk = jnp.where(strict, bt * _dot_nt(kb, kb) * decay, 0.0)
        tinv = _unit_lower_inverse(a_kk)
        gam = jnp.exp(gcc)
        rhs = jnp.concatenate([(bt * gam) * k, bt * v], axis=-1)
        sol = _dot3(_split_bf16(tinv), _split_bf16(rhs))
        w_mat, u_mat = sol[:, :HEAD_DIM], sol[:, HEAD_DIM:]
        q_dec = q * gam
        outs = []
        state = s_ref[h]
        for c in range(n_chunks):
            lo, hi = c * CHUNK, (c + 1) * CHUNK
            g_last = gcc[hi - 1:hi, :]
            qc = q[lo:hi].astype(BF16)
            a_qk = jnp.where(causal, _dot_nt(qc, kb[lo:hi]) * decay[lo:hi, lo:hi], 0.0)
            k_dec = k[lo:hi] * jnp.exp(g_last - gcc[lo:hi])
            sb = state.astype(BF16)
            delta = u_mat[lo:hi] - _dot(w_mat[lo:hi].astype(BF16), sb)
            db = delta.astype(BF16)
            outs.append(_dot(q_dec[lo:hi].astype(BF16), sb) + _dot(a_qk.astype(BF16), db))
            state = jnp.exp(g_last) * state + _dot_tn(k_dec.astype(BF16), db)
        s_ref[h] = state
        o = jnp.concatenate(outs, axis=0) if n_chunks > 1 else outs[0]
        zg = bz_ref[:, h * HEAD_DIM:(h + 1) * HEAD_DIM]
        o_ref[:, h * HEAD_DIM:(h + 1) * HEAD_DIM] = (
            _rms(o, gn_ref[...]) * (zg * _sigmoid(zg))).astype(o_ref.dtype)


def _dot_nt_highest(a, b):
    return lax.dot_general(a, b, (((1,), (1,)), ((), ())), precision=HIGHEST,
                           preferred_element_type=F32)


def _gdn(z, zs, zst, conv_w, a_log, dt_bias, norm_gain, t):
    seq = z.shape[0]
    rows = t["r_rec"]
    qkv_w = 3 * BRANCH_WIDTH
    full = lambda shape: pl.BlockSpec(shape, lambda i: (0,) * len(shape))
    return pl.pallas_call(
        functools.partial(_gdn_body, rows=rows),
        grid=(seq // rows,),
        in_specs=[
            pl.BlockSpec((rows, qkv_w), lambda i: (i, COL_GDN_QKV // qkv_w)),
            pl.BlockSpec((rows, BRANCH_WIDTH), lambda i: (i, COL_GDN_Z // BRANCH_WIDTH)),
            pl.BlockSpec((rows, LANES), lambda i: (i, 0)),
            pl.BlockSpec((SM_ROWS, rows), lambda i: (0, i)),
            full((GDN_CONV, qkv_w)),
            full((1, N_HEADS)), full((N_HEADS, 1)), full((1, N_HEADS)), full((N_HEADS, 1)),
            full((1, HEAD_DIM)),
        ],
        out_specs=pl.BlockSpec((rows, BRANCH_WIDTH), lambda i: (i, 0)),
        out_shape=jax.ShapeDtypeStruct((seq, BRANCH_WIDTH), BF16),
        scratch_shapes=[
            pltpu.VMEM((rows + SUBLANES, qkv_w), F32),
            pltpu.VMEM((N_HEADS, HEAD_DIM, HEAD_DIM), F32),
        ],
        compiler_params=_cparams("arbitrary"),
        name="gated_deltanet",
    )(z, z, zs, zst, conv_w, a_log.reshape(1, -1), a_log.reshape(-1, 1),
      dt_bias.reshape(1, -1), dt_bias.reshape(-1, 1), norm_gain.reshape(1, -1))


def _gla_body(q_ref, k_ref, v_ref, g_ref, zs_ref, wgk_ref, bgk_ref, gn_ref, o_ref, s_ref, *, rows):
    @pl.when(pl.program_id(0) == 0)
    def _():
        s_ref[...] = jnp.zeros(s_ref.shape, F32)

    lg = _log_sigmoid(_dot(zs_ref[...].astype(BF16), wgk_ref[...]) + bgk_ref[...])
    lg = lg * (1.0 / GLA_GATE_NORMALIZER)
    rr, cc = _iota((rows, rows), 0), _iota((rows, rows), 1)
    shift = int(math.log2(CHUNK))
    same = lax.shift_right_logical(rr, shift) == lax.shift_right_logical(cc, shift)
    incl = (same & (cc <= rr)).astype(F32)
    gcs = jnp.dot(incl, lg, precision=HIGHEST, preferred_element_type=F32)
    rc, ccn = _iota((CHUNK, CHUNK), 0), _iota((CHUNK, CHUNK), 1)
    causal = ccn <= rc
    n_chunks = rows // CHUNK
    qa = q_ref[...] * (GLA_HEAD_K ** -0.5)
    ka = k_ref[...]

    for h in range(N_HEADS):
        ks = slice(h * GLA_HEAD_K, (h + 1) * GLA_HEAD_K)
        vs = slice(h * HEAD_DIM, (h + 1) * HEAD_DIM)
        state = s_ref[h]
        outs = []
        for c in range(n_chunks):
            lo, hi = c * CHUNK, (c + 1) * CHUNK
            q, k, g = qa[lo:hi, ks], ka[lo:hi, ks], gcs[lo:hi, ks]
            v = v_ref[lo:hi, vs].astype(BF16)
            g_last = g[CHUNK - 1:CHUNK, :]
            blocks = []
            for sblk in range(CHUNK // GLA_SUB):
                a, b = sblk * GLA_SUB, (sblk + 1) * GLA_SUB
                ref = g[a:a + 1, :]
                qt = q[a:b] * jnp.exp(g[a:b] - ref)
                kt = k * jnp.exp(jnp.minimum(ref - g, 80.0))
                blocks.append(_dot_nt(qt.astype(BF16), kt.astype(BF16)))
            a_mat = jnp.where(causal, jnp.concatenate(blocks, axis=0), 0.0)
            inter = _dot_nt((q * jnp.exp(g)).astype(BF16), state.astype(BF16))
            outs.append(inter + _dot(a_mat.astype(BF16), v))
            k_dec = k * jnp.exp(g_last - g)
            state = jnp.exp(g_last) * state + _dot_tn(v, k_dec.astype(BF16))
        s_ref[h] = state
        o = jnp.concatenate(outs, axis=0) if n_chunks > 1 else outs[0]
        gate = g_ref[:, vs]
        o_ref[:, vs] = (_rms(o, gn_ref[...]) * (gate * _sigmoid(gate))).astype(o_ref.dtype)


def _gla(z, zs, w_gk_pad, b_gk, norm_gain, t):
    seq = z.shape[0]
    rows = t["r_rec"]
    full = lambda shape: pl.BlockSpec(shape, lambda i: (0,) * len(shape))
    return pl.pallas_call(
        functools.partial(_gla_body, rows=rows),
        grid=(seq // rows,),
        in_specs=[
            pl.BlockSpec((rows, GLA_K_WIDTH), lambda i: (i, COL_GLA_Q // GLA_K_WIDTH)),
            pl.BlockSpec((rows, GLA_K_WIDTH), lambda i: (i, COL_GLA_K // GLA_K_WIDTH)),
            pl.BlockSpec((rows, BRANCH_WIDTH), lambda i: (i, COL_GLA_V // BRANCH_WIDTH)),
            pl.BlockSpec((rows, BRANCH_WIDTH), lambda i: (i, COL_GLA_G // BRANCH_WIDTH)),
            pl.BlockSpec((rows, LANES), lambda i: (i, 0)),
            full((LANES, GLA_K_WIDTH)), full((1, GLA_K_WIDTH)), full((1, HEAD_DIM)),
        ],
        out_specs=pl.BlockSpec((rows, BRANCH_WIDTH), lambda i: (i, 0)),
        out_shape=jax.ShapeDtypeStruct((seq, BRANCH_WIDTH), BF16),
        scratch_shapes=[pltpu.VMEM((N_HEADS, HEAD_DIM, GLA_HEAD_K), F32)],
        compiler_params=_cparams("arbitrary"),
        name="gla",
    )(z, z, z, z, zs, w_gk_pad, b_gk.reshape(1, -1), norm_gain.reshape(1, -1))


S5_PAIR_IN = 2 * S5_T * S5_GROUP_DIM
S5_PAIR_STATE = 2 * S5_STATE
S5_STATE_WIDTH = S5_GROUPS * S5_STATE


def _s5_matrices(a_re, a_im, b_re, b_im, c_re, c_im, d, log_dt):
    g, p, hg, tt = S5_GROUPS, S5_STATE, S5_GROUP_DIM, S5_T
    lam = lax.complex(a_re.astype(F32), a_im.astype(F32))
    ldt = lam * jnp.exp(log_dt.astype(F32))[:, None]
    lam_bar = jnp.exp(ldt)
    b_bar = ((lam_bar - 1.0) / lam)[..., None] * lax.complex(b_re.astype(F32), b_im.astype(F32))
    c = lax.complex(c_re.astype(F32), c_im.astype(F32))
    steps = jnp.arange(tt + 1, dtype=F32)
    pw = jnp.exp(ldt[None] * steps[:, None, None])
    k_lag = jnp.einsum("ghp,lgp,gpk->lghk", c, pw[:tt], b_bar, precision=HIGHEST).real
    tau = jnp.arange(tt)
    lag = tau[None, :] - tau[:, None]
    kt = jnp.where((lag >= 0)[:, :, None, None, None], k_lag[jnp.clip(lag, 0)], 0.0)
    k_intra = kt.transpose(2, 0, 4, 1, 3).reshape(g, tt * hg, tt * hg)
    d_diag = jnp.tile(d.astype(F32).reshape(g, 1, hg), (1, tt, 1)).reshape(g, tt * hg)
    k_intra = k_intra + jnp.eye(tt * hg, dtype=F32)[None] * d_diag[:, None, :]
    w = pw[tt - 1 - tau][:, :, :, None] * b_bar[None]
    w = w.transpose(1, 0, 3, 2).reshape(g, tt * hg, p)
    ko = c[None] * pw[1:tt + 1][:, :, None, :]
    ko = ko.transpose(1, 3, 0, 2).reshape(g, p, tt * hg)

    def pair(m):
        a, b = m.shape[1:]
        m = m.reshape(g // 2, 2, a, b)
        z = jnp.zeros((g // 2, a, b), F32)
        top = jnp.concatenate([m[:, 0], z], axis=2)
        bot = jnp.concatenate([z, m[:, 1]], axis=2)
        return jnp.concatenate([top, bot], axis=1)

    lam_t = pw[tt].reshape(1, g * p)
    return dict(k_intra=pair(k_intra), w_re=pair(w.real), w_im=pair(w.imag),
                ko_re=pair(ko.real), ko_im=pair(-ko.imag),
                lam_re=lam_t.real, lam_im=lam_t.imag)


def _s5_local_body(u_ref, wre_ref, wim_ref, vre_ref, vim_ref):
    u = u_ref[...]
    vre_ref[...] = jnp.dot(u, wre_ref[...], precision=HIGHEST, preferred_element_type=F32)
    vim_ref[...] = jnp.dot(u, wim_ref[...], precision=HIGHEST, preferred_element_type=F32)


def _s5_scan_body(vre_ref, vim_ref, lre_ref, lim_ref, xre_ref, xim_ref, sre_ref, sim_ref, *, tn):
    @pl.when(pl.program_id(0) == 0)
    def _():
        sre_ref[...] = jnp.zeros(sre_ref.shape, F32)
        sim_ref[...] = jnp.zeros(sim_ref.shape, F32)

    lre, lim = lre_ref[...], lim_ref[...]

    def body(m, carry):
        xr, xi = carry
        base = pl.multiple_of(m * SUBLANES, SUBLANES)
        vr = vre_ref[pl.ds(base, SUBLANES), :]
        vi = vim_ref[pl.ds(base, SUBLANES), :]
        out_r, out_i = [], []
        for j in range(SUBLANES):
            out_r.append(xr)
            out_i.append(xi)
            xr, xi = (lre * xr - lim * xi + vr[j:j + 1], lre * xi + lim * xr + vi[j:j + 1])
        xre_ref[pl.ds(base, SUBLANES), :] = jnp.concatenate(out_r, axis=0)
        xim_ref[pl.ds(base, SUBLANES), :] = jnp.concatenate(out_i, axis=0)
        return xr, xi

    xr, xi = lax.fori_loop(0, tn // SUBLANES, body, (sre_ref[...], sim_ref[...]))
    sre_ref[...] = xr
    sim_ref[...] = xi


def _gelu_tanh(x):
    return 0.5 * x * (1.0 + jnp.tanh(math.sqrt(2.0 / math.pi) * (x + 0.044715 * (x * x * x))))


def _s5_out_body(u_ref, xre_ref, xim_ref, ki_ref, kore_ref, koim_ref, y_ref):
    y = jnp.dot(u_ref[...], ki_ref[...], precision=HIGHEST, preferred_element_type=F32)
    y = y + jnp.dot(xre_ref[...], kore_ref[...], precision=HIGHEST, preferred_element_type=F32)
    y = y + jnp.dot(xim_ref[...], koim_ref[...], precision=HIGHEST, preferred_element_type=F32)
    y_ref[...] = _gelu_tanh(y)


def _glu_body(y_ref, w_ref, o_ref):
    y = y_ref[...]
    o_ref[...] = (y * _sigmoid(_dot(y.astype(BF16), w_ref[...]))).astype(o_ref.dtype)


def _s5(z, mats, w_glu, t):
    seq = z.shape[0]
    nb = seq // S5_T
    g, hg = S5_GROUPS, S5_GROUP_DIM
    n_pairs = g // 2
    u = z[:, COL_S5:COL_S5 + BRANCH_WIDTH].reshape(nb, S5_T, g, hg).transpose(0, 2, 1, 3)
    u = u.reshape(nb, g * S5_T * hg)
    pair_mat = lambda a, b: pl.BlockSpec((None, a, b), lambda j: (j, 0, 0))
    v_re, v_im = pl.pallas_call(
        _s5_local_body,
        grid=(n_pairs,),
        in_specs=[pl.BlockSpec((nb, S5_PAIR_IN), lambda j: (0, j)),
                  pair_mat(S5_PAIR_IN, S5_PAIR_STATE), pair_mat(S5_PAIR_IN, S5_PAIR_STATE)],
        out_specs=[pl.BlockSpec((nb, S5_PAIR_STATE), lambda j: (0, j))] * 2,
        out_shape=[jax.ShapeDtypeStruct((nb, S5_STATE_WIDTH), F32)] * 2,
        compiler_params=_cparams("arbitrary"),
        name="s5_local",
    )(u, mats["w_re"], mats["w_im"])

    tn = t["tn_scan"]
    rowblk = pl.BlockSpec((tn, S5_STATE_WIDTH), lambda i: (i, 0))
    vec = pl.BlockSpec((1, S5_STATE_WIDTH), lambda i: (0, 0))
    x_re, x_im = pl.pallas_call(
        functools.partial(_s5_scan_body, tn=tn),
        grid=(nb // tn,),
        in_specs=[rowblk, rowblk, vec, vec],
        out_specs=[rowblk, rowblk],
        out_shape=[jax.ShapeDtypeStruct((nb, S5_STATE_WIDTH), F32)] * 2,
        scratch_shapes=[pltpu.VMEM((1, S5_STATE_WIDTH), F32)] * 2,
        compiler_params=_cparams("arbitrary"),
        name="s5_scan",
    )(v_re, v_im, mats["lam_re"], mats["lam_im"])

    y = pl.pallas_call(
        _s5_out_body,
        grid=(n_pairs,),
        in_specs=[pl.BlockSpec((nb, S5_PAIR_IN), lambda j: (0, j)),
                  pl.BlockSpec((nb, S5_PAIR_STATE), lambda j: (0, j)),
                  pl.BlockSpec((nb, S5_PAIR_STATE), lambda j: (0, j)),
                  pair_mat(S5_PAIR_IN, S5_PAIR_IN),
                  pair_mat(S5_PAIR_STATE, S5_PAIR_IN), pair_mat(S5_PAIR_STATE, S5_PAIR_IN)],
        out_specs=pl.BlockSpec((nb, S5_PAIR_IN), lambda j: (0, j)),
        out_shape=jax.ShapeDtypeStruct((nb, g * S5_T * hg), F32),
        compiler_params=_cparams("arbitrary"),
        name="s5_out",
    )(u, x_re, x_im, mats["k_intra"], mats["ko_re"], mats["ko_im"])
    y = y.reshape(nb, g, S5_T, hg).transpose(0, 2, 1, 3).reshape(seq, BRANCH_WIDTH)

    tm = t["tm_glu"]
    return pl.pallas_call(
        _glu_body,
        grid=(seq // tm,),
        in_specs=[pl.BlockSpec((tm, BRANCH_WIDTH), lambda i: (i, 0)),
                  pl.BlockSpec((BRANCH_WIDTH, BRANCH_WIDTH), lambda i: (0, 0))],
        out_specs=pl.BlockSpec((tm, BRANCH_WIDTH), lambda i: (i, 0)),
        out_shape=jax.ShapeDtypeStruct((seq, BRANCH_WIDTH), BF16),
        compiler_params=_cparams("arbitrary"),
        name="s5_glu",
    )(y, w_glu)


def _merge_body(h_ref, y_ref, wg_ref, wb_ref, o_ref, acc_ref):
    b = pl.program_id(1)
    term = _sigmoid(_dot(h_ref[...], wg_ref[...])) * _dot(y_ref[...], wb_ref[...])

    @pl.when(b == 0)
    def _():
        acc_ref[...] = term

    @pl.when(b > 0)
    def _():
        acc_ref[...] += term

    @pl.when(b == N_BRANCHES - 1)
    def _():
        o_ref[...] = acc_ref[...].astype(o_ref.dtype)


def _merge(h, y, w_gate, w_branch, t):
    seq = h.shape[0]
    tm = t["tm_merge"]
    return pl.pallas_call(
        _merge_body,
        grid=(seq // tm, N_BRANCHES),
        in_specs=[
            pl.BlockSpec((tm, D_MODEL), lambda i, b: (i, 0)),
            pl.BlockSpec((None, tm, BRANCH_WIDTH), lambda i, b: (b, i, 0)),
            pl.BlockSpec((None, D_MODEL, D_MODEL), lambda i, b: (b, 0, 0)),
            pl.BlockSpec((None, BRANCH_WIDTH, D_MODEL), lambda i, b: (b, 0, 0)),
        ],
        out_specs=pl.BlockSpec((tm, D_MODEL), lambda i, b: (i, 0)),
        out_shape=jax.ShapeDtypeStruct((seq, D_MODEL), BF16),
        scratch_shapes=[pltpu.VMEM((tm, D_MODEL), F32)],
        compiler_params=_cparams("arbitrary", "arbitrary"),
        name="gated_merge",
    )(h, y, w_gate, w_branch)


def _out_body(m_ref, w_ref, x_ref, gpost_ref, gpre_ref, x1_ref, h2_ref):
    x1 = x_ref[...] + _rms(_dot(m_ref[...], w_ref[...]), gpost_ref[...])
    x1_ref[...] = x1
    h2_ref[...] = _rms(x1, gpre_ref[...]).astype(h2_ref.dtype)


def _out_proj(merged, w_out, x, g_post, g_pre, t):
    seq = x.shape[0]
    tm = t["tm_out"]
    row = pl.BlockSpec((tm, D_MODEL), lambda i: (i, 0))
    vec = pl.BlockSpec((1, D_MODEL), lambda i: (0, 0))
    return pl.pallas_call(
        _out_body,
        grid=(seq // tm,),
        in_specs=[row, pl.BlockSpec((D_MODEL, D_MODEL), lambda i: (0, 0)), row, vec, vec],
        out_specs=[row, row],
        out_shape=[jax.ShapeDtypeStruct((seq, D_MODEL), F32),
                   jax.ShapeDtypeStruct((seq, D_MODEL), BF16)],
        compiler_params=_cparams("arbitrary"),
        name="out_proj",
    )(merged, w_out, x, g_post, g_pre)


def _ffn_body(h_ref, wg_ref, wv_ref, cg_ref, cv_ref, wd_ref, x_ref, gp_ref, o_ref,
              acc_ref, carg_ref, carv_ref, *, tm):
    i, j = pl.program_id(0), pl.program_id(1)
    h = h_ref[...]

    def up_conv(w_ref, c_ref, car_ref):
        up = _dot(h, w_ref[...])
        prev = jnp.where(i == 0, 0.0, car_ref[j])
        car_ref[j] = up[tm - SUBLANES:tm, :]
        row = _iota(up.shape, 0)
        up1 = jnp.where(row == 0, prev[SUBLANES - 1:SUBLANES], pltpu.roll(up, 1, axis=0))
        up2 = jnp.where(row == 0, prev[SUBLANES - 2:SUBLANES - 1],
                        jnp.where(row == 1, prev[SUBLANES - 1:SUBLANES], pltpu.roll(up, 2, axis=0)))
        cw = c_ref[...]
        return cw[2:3] * up + cw[1:2] * up1 + cw[0:1] * up2

    gate = up_conv(wg_ref, cg_ref, carg_ref)
    val = up_conv(wv_ref, cv_ref, carv_ref)
    part = _dot((gate * _sigmoid(gate) * val).astype(BF16), wd_ref[...])

    @pl.when(j == 0)
    def _():
        acc_ref[...] = part

    @pl.when(j > 0)
    def _():
        acc_ref[...] += part

    @pl.when(j == pl.num_programs(1) - 1)
    def _():
        o_ref[...] = x_ref[...] + _rms(acc_ref[...], gp_ref[...])


def _ffn(h2, x1, w_gate, w_val, conv_gate, conv_val, w_down, g_post, t):
    seq = x1.shape[0]
    tm, tf = t["tm_ffn"], t["tf_ffn"]
    nf = D_FF_PAD // tf
    row = pl.BlockSpec((tm, D_MODEL), lambda i, j: (i, 0))
    return pl.pallas_call(
        functools.partial(_ffn_body, tm=tm),
        grid=(seq // tm, nf),
        in_specs=[
            row,
            pl.BlockSpec((D_MODEL, tf), lambda i, j: (0, j)),
            pl.BlockSpec((D_MODEL, tf), lambda i, j: (0, j)),
            pl.BlockSpec((FFN_CONV, tf), lambda i, j: (0, j)),
            pl.BlockSpec((FFN_CONV, tf), lambda i, j: (0, j)),
            pl.BlockSpec((tf, D_MODEL), lambda i, j: (j, 0)),
            row,
            pl.BlockSpec((1, D_MODEL), lambda i, j: (0, 0)),
        ],
        out_specs=row,
        out_shape=jax.ShapeDtypeStruct((seq, D_MODEL), F32),
        scratch_shapes=[
            pltpu.VMEM((tm, D_MODEL), F32),
            pltpu.VMEM((nf, SUBLANES, tf), F32),
            pltpu.VMEM((nf, SUBLANES, tf), F32),
        ],
        compiler_params=_cparams("arbitrary", "arbitrary"),
        name="conv_ffn",
    )(h2, w_gate, w_val, conv_gate, conv_val, w_down, x1, g_post)


def _layer_weights(w_in, gla_w_gk, w_branch, w_out, ffn_w_up, ffn_conv, ffn_w_down, s5_w_glu):
    o_ff, o_gdn, o_ba, o_bb, o_su = 1536, 1540, 3588, 3592, 3596
    o_dr, o_gate = 5644, 5660
    w_mix = jnp.concatenate(
        [w_in[:, :o_ff], w_in[:, o_gdn:o_ba], w_in[:, o_su:o_dr]], axis=1).astype(BF16)
    w_small = jnp.concatenate(
        [w_in[:, o_dr:o_gate], w_in[:, o_ff:o_gdn], w_in[:, o_ba:o_bb], w_in[:, o_bb:o_su],
         jnp.zeros((D_MODEL, LANES - 28), F32)], axis=1).astype(BF16)
    w_gate = w_in[:, o_gate:].reshape(D_MODEL, N_BRANCHES, D_MODEL).transpose(1, 0, 2).astype(BF16)
    w_gk_pad = jnp.concatenate(
        [gla_w_gk, jnp.zeros((LANES - GLA_RANK, GLA_K_WIDTH), F32)], axis=0).astype(BF16)
    pad = D_FF_PAD - D_FF
    padc = lambda a: jnp.pad(a, ((0, 0), (0, pad)))
    return dict(
        w_mix=w_mix, w_small=w_small, w_gate=w_gate, w_gk_pad=w_gk_pad,
        w_branch=w_branch.astype(BF16), w_out=w_out.astype(BF16),
        w_glu=s5_w_glu.astype(BF16),
        ffn_wg=padc(ffn_w_up[:, :D_FF]).astype(BF16), ffn_wv=padc(ffn_w_up[:, D_FF:]).astype(BF16),
        ffn_cg=padc(ffn_conv[:, :D_FF]), ffn_cv=padc(ffn_conv[:, D_FF:]),
        ffn_wd=jnp.pad(ffn_w_down, ((0, pad), (0, 0))).astype(BF16),
    )


def _layer(x, p, t):
    w = _layer_weights(p["w_in"], p["gla_w_gk"], p["w_branch"], p["w_out"], p["ffn_w_up"],
                       p["ffn_conv"], p["ffn_w_down"], p["s5_w_glu"])
    row = lambda a: a.reshape(1, -1)
    z, zs, h = _in_proj(x, row(p["norm_mix_pre"]), w["w_mix"], w["w_small"], t)
    zst = zs[:, :SM_ROWS].T
    c = _forget_cumsum(zst[SM_FF:SM_FF + N_HEADS], p["fox_f_bias"])
    y_a = _fox_attention(z, c, t)
    y_b = _gdn(z, zs, zst, p["gdn_conv"], p["gdn_a_log"], p["gdn_dt_bias"], p["gdn_norm"], t)
    mats = _s5_matrices(p["s5_a_re"], p["s5_a_im"], p["s5_b_re"], p["s5_b_im"], p["s5_c_re"],
                        p["s5_c_im"], p["s5_d"], p["s5_log_dt"])
    y_c = _s5(z, mats, w["w_glu"], t)
    y_d = _gla(z, zs, w["w_gk_pad"], p["gla_b_gk"], p["gla_norm"], t)
    merged = _merge(h, jnp.stack([y_a, y_b, y_c, y_d]), w["w_gate"], w["w_branch"], t)
    x1, h2 = _out_proj(merged, w["w_out"], x, row(p["norm_mix_post"]), row(p["norm_ffn_pre"]), t)
    return _ffn(h2, x1, w["ffn_wg"], w["ffn_wv"], w["ffn_cg"], w["ffn_cv"], w["ffn_wd"],
                row(p["norm_ffn_post"]), t)


def kernel(x, norm_mix_pre, norm_mix_post, norm_ffn_pre, norm_ffn_post, w_in, fox_f_bias, gdn_conv, gdn_a_log, gdn_dt_bias, gdn_norm, s5_a_re, s5_a_im, s5_b_re, s5_b_im, s5_c_re, s5_c_im, s5_d, s5_log_dt, s5_w_glu, gla_w_gk, gla_b_gk, gla_norm, w_branch, w_out, ffn_w_up, ffn_conv, ffn_w_down):
    params = dict(
        norm_mix_pre=norm_mix_pre, norm_mix_post=norm_mix_post, norm_ffn_pre=norm_ffn_pre,
        norm_ffn_post=norm_ffn_post, w_in=w_in, fox_f_bias=fox_f_bias, gdn_conv=gdn_conv,
        gdn_a_log=gdn_a_log, gdn_dt_bias=gdn_dt_bias, gdn_norm=gdn_norm, s5_a_re=s5_a_re,
        s5_a_im=s5_a_im, s5_b_re=s5_b_re, s5_b_im=s5_b_im, s5_c_re=s5_c_re, s5_c_im=s5_c_im,
        s5_d=s5_d, s5_log_dt=s5_log_dt, s5_w_glu=s5_w_glu, gla_w_gk=gla_w_gk, gla_b_gk=gla_b_gk,
        gla_norm=gla_norm, w_branch=w_branch, w_out=w_out, ffn_w_up=ffn_w_up, ffn_conv=ffn_conv,
        ffn_w_down=ffn_w_down)
    batch, seq, _ = x.shape
    assert batch == 1
    t = _tiles(seq)
    xs = x[0]
    for layer in range(w_in.shape[0]):
        xs = _layer(xs, {k: v[layer] for k, v in params.items()}, t)
    return xs[None]
```

```python
import functools
import math

import jax
import jax.numpy as jnp
from jax import lax
from jax.experimental import pallas as pl
from jax.experimental.pallas import tpu as pltpu

F32 = jnp.float32
BF16 = jnp.bfloat16
HIGHEST = lax.Precision.HIGHEST

D_MODEL = 2048
N_HEADS = 4
HEAD_DIM = 128
BRANCH_WIDTH = 512
GDN_CONV = 4
CHUNK = 64
GDN_SOLVE = 2 * CHUNK
GLA_HEAD_K = 64
GLA_K_WIDTH = 256
GLA_RANK = 16
GLA_GATE_NORMALIZER = 16.0
GLA_SUB = 16
S5_GROUPS = 32
S5_GROUP_DIM = 16
S5_STATE = 64
S5_T = 16
D_FF = 5504
D_FF_PAD = 5632
FFN_CONV = 3
N_BRANCHES = 4
NORM_EPS = 1e-6

LANES = 128
SUBLANES = 8
VMEM_LIMIT_BYTES = 56 * 1024 * 1024

MIX_WIDTH = 5632
COL_FOX = 0
COL_GDN_QKV = 1536
COL_GDN_Z = 3072
COL_S5 = 3584
COL_GLA_Q = 4096
COL_GLA_K = 4352
COL_GLA_V = 4608
COL_GLA_G = 5120
SM_DR = 0
SM_FF = 16
SM_BA = 20
SM_BB = 24
SM_ROWS = 32


def _tiles(seq):
    return dict(
        tm_in=min(512, seq), tn_in=2816,
        tn_s5=min(256, seq // S5_T),
        t_att=min(1024, seq), strip_att=128,
        r_rec=min(256, seq),
        r_gdn=min(256, seq),
        tm_merge=min(512, seq),
        tm_out=min(512, seq),
        tm_ffn=min(512, seq), tf_ffn=512,
        tm_glu=min(1024, seq),
        tn_scan=min(128, seq // S5_T),
    )


def _cparams(*sem):
    return pltpu.CompilerParams(dimension_semantics=sem, vmem_limit_bytes=VMEM_LIMIT_BYTES)


def _sigmoid(x):
    return 1.0 / (1.0 + jnp.exp(-x))


def _softplus(x):
    return jnp.maximum(x, 0.0) + jnp.log1p(jnp.exp(-jnp.abs(x)))


def _log_sigmoid(x):
    return jnp.minimum(x, 0.0) - jnp.log1p(jnp.exp(-jnp.abs(x)))


def _rms(x, gain):
    return x * lax.rsqrt(jnp.mean(x * x, axis=-1, keepdims=True) + NORM_EPS) * gain


def _dot(a, b):
    return jnp.dot(a, b, preferred_element_type=F32)


def _dot_nt(a, b):
    return lax.dot_general(a, b, (((1,), (1,)), ((), ())), preferred_element_type=F32)


def _dot_tn(a, b):
    return lax.dot_general(a, b, (((0,), (0,)), ((), ())), preferred_element_type=F32)


def _split_bf16(a):
    hi = a.astype(BF16)
    lo = (a - hi.astype(F32)).astype(BF16)
    return hi, lo


def _dot3(a_split, b_split):
    ah, al = a_split
    bh, bl = b_split
    return _dot(ah, bh) + (_dot(ah, bl) + _dot(al, bh))


def _iota(shape, axis):
    return lax.broadcasted_iota(jnp.int32, shape, axis)


def _in_proj_body(x_ref, g_ref, w_ref, ws_ref, z_ref, zs_ref, zst_ref, h_ref):
    @pl.when(pl.program_id(1) == 0)
    def _():
        h = _rms(x_ref[...], g_ref[...]).astype(BF16)
        h_ref[...] = h
        zs = _dot(h, ws_ref[...])
        zs_ref[...] = zs
        zst_ref[...] = zs.T[:SM_ROWS, :]

    z_ref[...] = _dot(h_ref[...], w_ref[...])


def _in_proj(x, gain, w_mix, w_small, t):
    seq = x.shape[0]
    tm, tn = t["tm_in"], t["tn_in"]
    return pl.pallas_call(
        _in_proj_body,
        grid=(seq // tm, MIX_WIDTH // tn),
        in_specs=[
            pl.BlockSpec((tm, D_MODEL), lambda i, j: (i, 0)),
            pl.BlockSpec((1, D_MODEL), lambda i, j: (0, 0)),
            pl.BlockSpec((D_MODEL, tn), lambda i, j: (0, j)),
            pl.BlockSpec((D_MODEL, LANES), lambda i, j: (0, 0)),
        ],
        out_specs=[
            pl.BlockSpec((tm, tn), lambda i, j: (i, j)),
            pl.BlockSpec((tm, LANES), lambda i, j: (i, 0)),
            pl.BlockSpec((SM_ROWS, tm), lambda i, j: (0, i)),
            pl.BlockSpec((tm, D_MODEL), lambda i, j: (i, 0)),
        ],
        out_shape=[
            jax.ShapeDtypeStruct((seq, MIX_WIDTH), F32),
            jax.ShapeDtypeStruct((seq, LANES), F32),
            jax.ShapeDtypeStruct((SM_ROWS, seq), F32),
            jax.ShapeDtypeStruct((seq, D_MODEL), BF16),
        ],
        compiler_params=_cparams("arbitrary", "arbitrary"),
        name="in_proj",
    )(x, gain, w_mix, w_small)


def _cum_body(ff_ref, b_ref, c_ref, *, rows):
    lf = _log_sigmoid(ff_ref[...] + b_ref[...]).reshape(N_HEADS * rows, LANES)
    upper = (_iota((LANES, LANES), 0) <= _iota((LANES, LANES), 1)).astype(F32)
    loc = jnp.dot(lf, upper, precision=HIGHEST, preferred_element_type=F32)
    tot = jnp.broadcast_to(loc[:, LANES - 1:LANES], loc.shape)
    n = N_HEADS * rows
    rr, cc = _iota((n, n), 0), _iota((n, n), 1)
    shift = int(math.log2(rows))
    same_head = lax.shift_right_logical(rr, shift) == lax.shift_right_logical(cc, shift)
    low = ((cc < rr) & same_head).astype(F32)
    off = jnp.dot(low, tot, precision=HIGHEST, preferred_element_type=F32)
    c_ref[...] = (loc + off).reshape(N_HEADS, rows, LANES)


def _forget_cumsum(ff_t, bias):
    seq = ff_t.shape[1]
    rows = seq // LANES
    assert rows & (rows - 1) == 0 and rows % SUBLANES == 0
    c = pl.pallas_call(
        functools.partial(_cum_body, rows=rows),
        out_shape=jax.ShapeDtypeStruct((N_HEADS, rows, LANES), F32),
        compiler_params=pltpu.CompilerParams(vmem_limit_bytes=VMEM_LIMIT_BYTES),
        name="fox_cumsum",
    )(ff_t.reshape(N_HEADS, rows, LANES), bias.reshape(N_HEADS, 1, 1))
    return c.reshape(N_HEADS, seq)


def _fox_body(qi_ref, kj_ref, q_ref, k_ref, v_ref, cq_ref, ck_ref, o_ref,
              m_ref, l_ref, acc_ref, qs_ref, *, strip):
    p = pl.program_id(1)
    qi, kj = qi_ref[p], kj_ref[p]

    @pl.when(kj == 0)
    def _():
        m_ref[...] = jnp.full(m_ref.shape, -jnp.inf, F32)
        l_ref[...] = jnp.zeros(l_ref.shape, F32)
        acc_ref[...] = jnp.zeros(acc_ref.shape, F32)
        qs_ref[...] = (q_ref[...] * (HEAD_DIM ** -0.5)).astype(BF16)

    def step(diagonal):
        kb = k_ref[...].astype(BF16)
        vb = v_ref[...].astype(BF16)
        bias = cq_ref[0, :, 0:1] - ck_ref[0]
        tq = qs_ref.shape[0]
        n_strips = tq // strip
        n_keys = lambda r: (r + 1) * strip if diagonal else kb.shape[0]

        def scores(r):
            nk = n_keys(r)
            s = _dot_nt(qs_ref[r * strip:(r + 1) * strip, :], kb[:nk]) + bias[:, :nk]
            if diagonal:
                s = jnp.where(_iota(s.shape, 1) <= _iota(s.shape, 0) + r * strip, s, -jnp.inf)
            return s

        s_next = scores(0)
        for r in range(n_strips):
            rows = slice(r * strip, (r + 1) * strip)
            nk = n_keys(r)
            s = s_next
            if r + 1 < n_strips:
                s_next = scores(r + 1)
            m_prev = m_ref[rows, :]
            m_new = jnp.maximum(m_prev, jnp.max(s, axis=-1, keepdims=True))
            alpha = jnp.exp(m_prev - m_new)
            e = jnp.exp(s - m_new)
            l_ref[rows, :] = alpha * l_ref[rows, :] + jnp.sum(e, axis=-1, keepdims=True)
            acc_ref[rows, :] = alpha * acc_ref[rows, :] + _dot(e.astype(BF16), vb[:nk])
            m_ref[rows, :] = m_new

    @pl.when(kj < qi)
    def _():
        step(False)

    @pl.when(kj == qi)
    def _():
        step(True)
        o_ref[...] = (acc_ref[...] / l_ref[...]).astype(o_ref.dtype)


def _fox_attention(z, c, t):
    seq = z.shape[0]
    tq = t["t_att"]
    nq = seq // tq
    pairs = [(i, j) for i in range(nq) for j in range(i + 1)]
    qi = jnp.asarray([a for a, _ in pairs], jnp.int32)
    kj = jnp.asarray([b for _, b in pairs], jnp.int32)
    cb = COL_FOX // HEAD_DIM
    grid_spec = pltpu.PrefetchScalarGridSpec(
        num_scalar_prefetch=2,
        grid=(N_HEADS, len(pairs)),
        in_specs=[
            pl.BlockSpec((tq, HEAD_DIM), lambda h, p, qi, kj: (qi[p], cb + h)),
            pl.BlockSpec((tq, HEAD_DIM), lambda h, p, qi, kj: (kj[p], cb + N_HEADS + h)),
            pl.BlockSpec((tq, HEAD_DIM), lambda h, p, qi, kj: (kj[p], cb + 2 * N_HEADS + h)),
            pl.BlockSpec((1, 1, tq), lambda h, p, qi, kj: (h, 0, qi[p])),
            pl.BlockSpec((1, 1, tq), lambda h, p, qi, kj: (h, 0, kj[p])),
        ],
        out_specs=pl.BlockSpec((tq, HEAD_DIM), lambda h, p, qi, kj: (qi[p], h)),
        scratch_shapes=[
            pltpu.VMEM((tq, 1), F32),
            pltpu.VMEM((tq, 1), F32),
            pltpu.VMEM((tq, HEAD_DIM), F32),
            pltpu.VMEM((tq, HEAD_DIM), BF16),
        ],
    )
    c3 = c.reshape(N_HEADS, 1, seq)
    return pl.pallas_call(
        functools.partial(_fox_body, strip=min(t["strip_att"], tq)),
        grid_spec=grid_spec,
        out_shape=jax.ShapeDtypeStruct((seq, BRANCH_WIDTH), BF16),
        compiler_params=_cparams("arbitrary", "arbitrary"),
        name="fox_attention",
    )(qi, kj, z, z, z, c3, c3)


def _unit_lower_inverses(mats):
    n = mats[0].shape[0]
    eye = (_iota((n, n), 0) == _iota((n, n), 1)).astype(F32)
    ps = [eye - a for a in mats]
    ms = [_split_bf16(a) for a in mats]
    for _ in range(int(math.log2(CHUNK)) - 1):
        ms = [_split_bf16(_dot3(m, m)) for m in ms]
        ps = [p + _dot3(_split_bf16(p), m) for p, m in zip(ps, ms)]
    return ps


def _gdn_body(qkv_ref, bz_ref, zs_ref, zst_ref, cw_ref, alr_ref, alc_ref, dtr_ref, dtc_ref,
              gn_ref, o_ref, ext_ref, s_ref, *, rows):
    @pl.when(pl.program_id(0) == 0)
    def _():
        ext_ref[0:SUBLANES, :] = jnp.zeros((SUBLANES, ext_ref.shape[1]), F32)
        s_ref[...] = jnp.zeros(s_ref.shape, F32)

    x = qkv_ref[...]
    ext_ref[SUBLANES:SUBLANES + rows, :] = x
    cw = cw_ref[...]
    conv = cw[GDN_CONV - 1:GDN_CONV] * x
    for j in range(GDN_CONV - 1):
        off = SUBLANES - (GDN_CONV - 1) + j
        conv = conv + cw[j:j + 1] * ext_ref[off:off + rows, :]
    ext_ref[0:SUBLANES, :] = x[rows - SUBLANES:rows, :]
    qkv = conv * _sigmoid(conv)

    zs = zs_ref[...]
    zst = zst_ref[...]
    g_col = -jnp.exp(alr_ref[...]) * _softplus(zs[:, SM_BA:SM_BA + N_HEADS] + dtr_ref[...])
    g_row = -jnp.exp(alc_ref[...]) * _softplus(zst[SM_BA:SM_BA + N_HEADS, :] + dtc_ref[...])
    beta = _sigmoid(zs[:, SM_BB:SM_BB + N_HEADS])
    rr, cc = _iota((rows, rows), 0), _iota((rows, rows), 1)
    shift = int(math.log2(CHUNK))
    same = lax.shift_right_logical(rr, shift) == lax.shift_right_logical(cc, shift)
    incl = (same & (cc <= rr)).astype(F32)
    gc_col = jnp.dot(incl, g_col, precision=HIGHEST, preferred_element_type=F32)
    gc_row = _dot_nt_highest(g_row, incl)
    strict = (same & (cc < rr))[:GDN_SOLVE, :GDN_SOLVE]
    n_chunks = rows // CHUNK
    rc, ccn = _iota((CHUNK, CHUNK), 0), _iota((CHUNK, CHUNK), 1)
    causal = ccn <= rc

    heads = range(N_HEADS)
    blocks = [slice(b * GDN_SOLVE, (b + 1) * GDN_SOLVE) for b in range(rows // GDN_SOLVE)]
    hs = lambda base, h: slice(base + h * HEAD_DIM, base + (h + 1) * HEAD_DIM)
    qs, ks, kbs, gccs, gcrs, rhss, a_kks = [], [], [], [], [], [], []
    for h in heads:
        q, k, v = qkv[:, hs(0, h)], qkv[:, hs(BRANCH_WIDTH, h)], qkv[:, hs(2 * BRANCH_WIDTH, h)]
        q = q * lax.rsqrt(jnp.sum(q * q, axis=-1, keepdims=True) + NORM_EPS) * (HEAD_DIM ** -0.5)
        k = k * lax.rsqrt(jnp.sum(k * k, axis=-1, keepdims=True) + NORM_EPS)
        gcc = gc_col[:, h:h + 1]
        gcr = gc_row[h:h + 1, :]
        bt = beta[:, h:h + 1]
        kb = k.astype(BF16)
        gam = jnp.exp(gcc)
        qs.append((q, q * gam))
        ks.append(k)
        kbs.append(kb)
        gccs.append(gcc)
        gcrs.append(gcr)
        rhss.append(jnp.concatenate([(bt * gam) * k, bt * v], axis=-1))
        for sl in blocks:
            decay = jnp.exp(jnp.minimum(gcc[sl] - gcr[:, sl], 0.0))
            a_kks.append(jnp.where(strict, bt[sl] * _dot_nt(kb[sl], kb[sl]) * decay, 0.0))
    tinvs = _unit_lower_inverses(a_kks)
    sols = []
    for h in heads:
        parts = [_dot3(_split_bf16(tinvs[h * len(blocks) + b]), _split_bf16(rhss[h][sl]))
                 for b, sl in enumerate(blocks)]
        sols.append(jnp.concatenate(parts, axis=0) if len(parts) > 1 else parts[0])
    states = [s_ref[h] for h in heads]
    outs = [[] for _ in heads]
    for c in range(n_chunks):
        lo, hi = c * CHUNK, (c + 1) * CHUNK
        for h in heads:
            (q, q_dec), k, kb, gcc, gcr = qs[h], ks[h], kbs[h], gccs[h], gcrs[h]
            w_mat, u_mat = sols[h][lo:hi, :HEAD_DIM], sols[h][lo:hi, HEAD_DIM:]
            g_last = gcc[hi - 1:hi, :]
            decay = jnp.exp(jnp.minimum(gcc[lo:hi] - gcr[:, lo:hi], 0.0))
            a_qk = jnp.where(causal, _dot_nt(q[lo:hi].astype(BF16), kb[lo:hi]) * decay, 0.0)
            k_dec = k[lo:hi] * jnp.exp(g_last - gcc[lo:hi])
            sb = states[h].astype(BF16)
            delta = u_mat - _dot(w_mat.astype(BF16), sb)
            db = delta.astype(BF16)
            outs[h].append(_dot(q_dec[lo:hi].astype(BF16), sb) + _dot(a_qk.astype(BF16), db))
            states[h] = jnp.exp(g_last) * states[h] + _dot_tn(k_dec.astype(BF16), db)
    for h in heads:
        s_ref[h] = states[h]
        o = jnp.concatenate(outs[h], axis=0) if n_chunks > 1 else outs[h][0]
        zg = bz_ref[:, h * HEAD_DIM:(h + 1) * HEAD_DIM]
        o_ref[:, h * HEAD_DIM:(h + 1) * HEAD_DIM] = (
            _rms(o, gn_ref[...]) * (zg * _sigmoid(zg))).astype(o_ref.dtype)


def _dot_nt_highest(a, b):
    return lax.dot_general(a, b, (((1,), (1,)), ((), ())), precision=HIGHEST,
                           preferred_element_type=F32)


def _gdn(z, zs, zst, conv_w, a_log, dt_bias, norm_gain, t):
    seq = z.shape[0]
    rows = t["r_gdn"]
    assert rows % GDN_SOLVE == 0
    qkv_w = 3 * BRANCH_WIDTH
    full = lambda shape: pl.BlockSpec(shape, lambda i: (0,) * len(shape))
    return pl.pallas_call(
        functools.partial(_gdn_body, rows=rows),
        grid=(seq // rows,),
        in_specs=[
            pl.BlockSpec((rows, qkv_w), lambda i: (i, COL_GDN_QKV // qkv_w)),
            pl.BlockSpec((rows, BRANCH_WIDTH), lambda i: (i, COL_GDN_Z // BRANCH_WIDTH)),
            pl.BlockSpec((rows, LANES), lambda i: (i, 0)),
            pl.BlockSpec((SM_ROWS, rows), lambda i: (0, i)),
            full((GDN_CONV, qkv_w)),
            full((1, N_HEADS)), full((N_HEADS, 1)), full((1, N_HEADS)), full((N_HEADS, 1)),
            full((1, HEAD_DIM)),
        ],
        out_specs=pl.BlockSpec((rows, BRANCH_WIDTH), lambda i: (i, 0)),
        out_shape=jax.ShapeDtypeStruct((seq, BRANCH_WIDTH), BF16),
        scratch_shapes=[
            pltpu.VMEM((rows + SUBLANES, qkv_w), F32),
            pltpu.VMEM((N_HEADS, HEAD_DIM, HEAD_DIM), F32),
        ],
        compiler_params=_cparams("arbitrary"),
        name="gated_deltanet",
    )(z, z, zs, zst, conv_w, a_log.reshape(1, -1), a_log.reshape(-1, 1),
      dt_bias.reshape(1, -1), dt_bias.reshape(-1, 1), norm_gain.reshape(1, -1))


def _gla_body(q_ref, k_ref, v_ref, g_ref, zs_ref, wgk_ref, bgk_ref, gn_ref, o_ref, s_ref, *, rows):
    @pl.when(pl.program_id(0) == 0)
    def _():
        s_ref[...] = jnp.zeros(s_ref.shape, F32)

    lg = _log_sigmoid(_dot(zs_ref[...].astype(BF16), wgk_ref[...]) + bgk_ref[...])
    lg = lg * (1.0 / GLA_GATE_NORMALIZER)
    rr, cc = _iota((rows, rows), 0), _iota((rows, rows), 1)
    shift = int(math.log2(CHUNK))
    same = lax.shift_right_logical(rr, shift) == lax.shift_right_logical(cc, shift)
    incl = (same & (cc <= rr)).astype(F32)
    gcs = jnp.dot(incl, lg, precision=HIGHEST, preferred_element_type=F32)
    rc, ccn = _iota((CHUNK, CHUNK), 0), _iota((CHUNK, CHUNK), 1)
    causal = ccn <= rc
    n_chunks = rows // CHUNK
    qa = q_ref[...] * (GLA_HEAD_K ** -0.5)
    ka = k_ref[...]

    heads = range(N_HEADS)
    k_slice = lambda h: slice(h * GLA_HEAD_K, (h + 1) * GLA_HEAD_K)
    v_slice = lambda h: slice(h * HEAD_DIM, (h + 1) * HEAD_DIM)
    pairs = [(c, h) for c in range(n_chunks) for h in heads]
    a_mats, kvs, q_decs, vbs, g_lasts = {}, {}, {}, {}, {}
    for c, h in pairs:
        lo, hi = c * CHUNK, (c + 1) * CHUNK
        q, k, g = qa[lo:hi, k_slice(h)], ka[lo:hi, k_slice(h)], gcs[lo:hi, k_slice(h)]
        v = v_ref[lo:hi, v_slice(h)].astype(BF16)
        g_last = g[CHUNK - 1:CHUNK, :]
        blocks = []
        for sblk in range(CHUNK // GLA_SUB):
            a, b = sblk * GLA_SUB, (sblk + 1) * GLA_SUB
            ref = g[a:a + 1, :]
            qt = q[a:b] * jnp.exp(g[a:b] - ref)
            kt = k * jnp.exp(jnp.minimum(ref - g, 80.0))
            blocks.append(_dot_nt(qt.astype(BF16), kt.astype(BF16)))
        a_mats[c, h] = jnp.where(causal, jnp.concatenate(blocks, axis=0), 0.0).astype(BF16)
        kvs[c, h] = _dot_tn(v, (k * jnp.exp(g_last - g)).astype(BF16))
        q_decs[c, h] = (q * jnp.exp(g)).astype(BF16)
        vbs[c, h], g_lasts[c, h] = v, g_last
    states = [s_ref[h] for h in heads]
    outs = [[] for _ in heads]
    for c, h in pairs:
        inter = _dot_nt(q_decs[c, h], states[h].astype(BF16))
        outs[h].append(inter + _dot(a_mats[c, h], vbs[c, h]))
        states[h] = jnp.exp(g_lasts[c, h]) * states[h] + kvs[c, h]
    for h in heads:
        vs = v_slice(h)
        s_ref[h] = states[h]
        o = jnp.concatenate(outs[h], axis=0) if n_chunks > 1 else outs[h][0]
        gate = g_ref[:, vs]
        o_ref[:, vs] = (_rms(o, gn_ref[...]) * (gate * _sigmoid(gate))).astype(o_ref.dtype)


def _gla(z, zs, w_gk_pad, b_gk, norm_gain, t):
    seq = z.shape[0]
    rows = t["r_rec"]
    full = lambda shape: pl.BlockSpec(shape, lambda i: (0,) * len(shape))
    return pl.pallas_call(
        functools.partial(_gla_body, rows=rows),
        grid=(seq // rows,),
        in_specs=[
            pl.BlockSpec((rows, GLA_K_WIDTH), lambda i: (i, COL_GLA_Q // GLA_K_WIDTH)),
            pl.BlockSpec((rows, GLA_K_WIDTH), lambda i: (i, COL_GLA_K // GLA_K_WIDTH)),
            pl.BlockSpec((rows, BRANCH_WIDTH), lambda i: (i, COL_GLA_V // BRANCH_WIDTH)),
            pl.BlockSpec((rows, BRANCH_WIDTH), lambda i: (i, COL_GLA_G // BRANCH_WIDTH)),
            pl.BlockSpec((rows, LANES), lambda i: (i, 0)),
            full((LANES, GLA_K_WIDTH)), full((1, GLA_K_WIDTH)), full((1, HEAD_DIM)),
        ],
        out_specs=pl.BlockSpec((rows, BRANCH_WIDTH), lambda i: (i, 0)),
        out_shape=jax.ShapeDtypeStruct((seq, BRANCH_WIDTH), BF16),
        scratch_shapes=[pltpu.VMEM((N_HEADS, HEAD_DIM, GLA_HEAD_K), F32)],
        compiler_params=_cparams("arbitrary"),
        name="gla",
    )(z, z, z, z, zs, w_gk_pad, b_gk.reshape(1, -1), norm_gain.reshape(1, -1))


S5_OCT = LANES // S5_GROUP_DIM
S5_N_OCT = S5_GROUPS // S5_OCT
S5_OCT_IN = S5_T * LANES
S5_OCT_STATE = S5_OCT * S5_STATE
S5_STATE_WIDTH = S5_GROUPS * S5_STATE


def _s5_matrices(a_re, a_im, b_re, b_im, c_re, c_im, d, log_dt):
    g, p, hg, tt, no, oc = S5_GROUPS, S5_STATE, S5_GROUP_DIM, S5_T, S5_N_OCT, S5_OCT
    lam = lax.complex(a_re.astype(F32), a_im.astype(F32))
    ldt = lam * jnp.exp(log_dt.astype(F32))[:, None]
    lam_bar = jnp.exp(ldt)
    b_bar = ((lam_bar - 1.0) / lam)[..., None] * lax.complex(b_re.astype(F32), b_im.astype(F32))
    c = lax.complex(c_re.astype(F32), c_im.astype(F32))
    steps = jnp.arange(tt + 1, dtype=F32)
    pw = jnp.exp(ldt[None] * steps[:, None, None])
    k_lag = jnp.einsum("ghp,lgp,gpk->gklh", c, pw[:tt], b_bar, precision=HIGHEST).real
    k_lag = k_lag.at[:, :, 0, :].add(jnp.eye(hg, dtype=F32)[None] * d.astype(F32).reshape(g, 1, hg))
    kc = k_lag.reshape(no, LANES, tt * hg)
    tau = jnp.arange(tt)
    w = pw[tt - 1 - tau][:, :, None, :] * b_bar.transpose(0, 2, 1)[None]
    w = jnp.stack([w.real, w.imag], axis=3)
    wc = w.reshape(tt, no, oc, hg, 2 * p).transpose(1, 0, 2, 3, 4).reshape(no, S5_OCT_IN, 2 * p)
    ko = c.transpose(0, 2, 1)[:, :, None, :] * pw[1:tt + 1].transpose(1, 2, 0)[:, :, :, None]
    ko = jnp.stack([ko.real, -ko.imag], axis=0)
    koc = ko.reshape(2, no, oc * p, tt * hg).transpose(1, 0, 2, 3).reshape(no, 2 * S5_OCT_STATE, tt * hg)
    lam_t = pw[tt].reshape(1, g * p)
    m_pad, w_cat, ko_cat = _s5_expand(kc, wc, koc)
    return dict(m_pad=m_pad, w=w_cat, ko=ko_cat, lam_re=lam_t.real, lam_im=lam_t.imag)


def _s5_expand_body(kc_ref, wc_ref, koc_ref, mpad_ref, w_ref, ko_ref):
    hg, p = S5_GROUP_DIM, S5_STATE
    sh = lambda x, n: lax.shift_right_logical(x, int(math.log2(n)))
    md = lambda x, n: jnp.bitwise_and(x, n - 1)

    def replicate(src, out_cols, unit):
        n_in = src.shape[1]
        a, c = _iota((n_in, out_cols), 0), _iota((n_in, out_cols), 1)
        rep = (sh(a, unit) == sh(c, unit * S5_OCT)) & (md(a, unit) == md(c, unit))
        return _dot(src.astype(BF16), rep.astype(BF16))

    def keep_diagonal(x, row_unit, col_unit):
        r, c = _iota(x.shape, 0), _iota(x.shape, 1)
        same = md(sh(r, row_unit), S5_OCT) == md(sh(c, col_unit), S5_OCT)
        return jnp.where(same, x, 0.0).astype(BF16)

    m = keep_diagonal(replicate(kc_ref[...], S5_OCT_IN, hg), hg, hg)
    mpad_ref[:, 0:(S5_T - 1) * LANES] = jnp.zeros((LANES, (S5_T - 1) * LANES), BF16)
    mpad_ref[:, (S5_T - 1) * LANES:] = m
    w_ref[...] = keep_diagonal(replicate(wc_ref[...], 2 * S5_OCT_STATE, p), hg, p)
    ko_ref[...] = keep_diagonal(replicate(koc_ref[...], S5_OCT_IN, hg), p, hg)


def _s5_expand(kc, wc, koc):
    no = S5_N_OCT
    shapes = ((LANES, (2 * S5_T - 1) * LANES), (S5_OCT_IN, 2 * S5_OCT_STATE),
              (2 * S5_OCT_STATE, S5_OCT_IN))
    spec = lambda a, b: pl.BlockSpec((None, a, b), lambda o: (o, 0, 0))
    return pl.pallas_call(
        _s5_expand_body,
        grid=(no,),
        in_specs=[spec(*kc.shape[1:]), spec(*wc.shape[1:]), spec(*koc.shape[1:])],
        out_specs=[spec(*s) for s in shapes],
        out_shape=[jax.ShapeDtypeStruct((no,) + s, BF16) for s in shapes],
        compiler_params=_cparams("arbitrary"),
        name="s5_expand",
    )(kc, wc, koc)


def _s5_gather(u_ref, tn):
    return jnp.concatenate(
        [u_ref[pl.ds(s, tn, stride=S5_T), :] for s in range(S5_T)], axis=1).astype(BF16)


def _s5_local_body(u_ref, w_ref, vre_ref, vim_ref, *, tn):
    v = _dot(_s5_gather(u_ref, tn), w_ref[...])
    vre_ref[...] = v[:, :S5_OCT_STATE]
    vim_ref[...] = v[:, S5_OCT_STATE:]


def _s5_scan_body(vre_ref, vim_ref, lre_ref, lim_ref, xre_ref, xim_ref, sre_ref, sim_ref, *, tn):
    @pl.when(pl.program_id(0) == 0)
    def _():
        sre_ref[...] = jnp.zeros(sre_ref.shape, F32)
        sim_ref[...] = jnp.zeros(sim_ref.shape, F32)

    lre, lim = lre_ref[...], lim_ref[...]

    def body(m, carry):
        xr, xi = carry
        base = pl.multiple_of(m * SUBLANES, SUBLANES)
        vr = vre_ref[pl.ds(base, SUBLANES), :]
        vi = vim_ref[pl.ds(base, SUBLANES), :]
        out_r, out_i = [], []
        for j in range(SUBLANES):
            out_r.append(xr)
            out_i.append(xi)
            xr, xi = (lre * xr - lim * xi + vr[j:j + 1], lre * xi + lim * xr + vi[j:j + 1])
        xre_ref[pl.ds(base, SUBLANES), :] = jnp.concatenate(out_r, axis=0)
        xim_ref[pl.ds(base, SUBLANES), :] = jnp.concatenate(out_i, axis=0)
        return xr, xi

    xr, xi = lax.fori_loop(0, tn // SUBLANES, body, (sre_ref[...], sim_ref[...]))
    sre_ref[...] = xr
    sim_ref[...] = xi


def _gelu_tanh(x):
    return 0.5 * x * (1.0 + jnp.tanh(math.sqrt(2.0 / math.pi) * (x + 0.044715 * (x * x * x))))


def _s5_out_body(u_ref, xre_ref, xim_ref, mpad_ref, ko_ref, y_ref, ki_ref, *, tn):
    @pl.when(pl.program_id(1) == 0)
    def _():
        for s in range(S5_T):
            off = (S5_T - 1 - s) * LANES
            ki_ref[s * LANES:(s + 1) * LANES, :] = mpad_ref[:, off:off + S5_OCT_IN]

    x = jnp.concatenate([xre_ref[...], xim_ref[...]], axis=1).astype(BF16)
    y = _gelu_tanh(_dot(_s5_gather(u_ref, tn), ki_ref[...]) + _dot(x, ko_ref[...]))
    for s in range(S5_T):
        y_ref[pl.ds(s, tn, stride=S5_T), :] = y[:, s * LANES:(s + 1) * LANES]


def _glu_body(y_ref, w_ref, o_ref):
    y = y_ref[...]
    o_ref[...] = (y * _sigmoid(_dot(y.astype(BF16), w_ref[...]))).astype(o_ref.dtype)


def _s5(z, mats, w_glu, t):
    seq = z.shape[0]
    nb = seq // S5_T
    tb = t["tn_s5"]
    col0 = COL_S5 // LANES
    u_spec = pl.BlockSpec((S5_T * tb, LANES), lambda o, n: (n, col0 + o))
    st_spec = pl.BlockSpec((tb, S5_OCT_STATE), lambda o, n: (n, o))
    oct_mat = lambda a, b: pl.BlockSpec((None, a, b), lambda o, n: (o, 0, 0))
    v_re, v_im = pl.pallas_call(
        functools.partial(_s5_local_body, tn=tb),
        grid=(S5_N_OCT, nb // tb),
        in_specs=[u_spec, oct_mat(S5_OCT_IN, 2 * S5_OCT_STATE)],
        out_specs=[st_spec, st_spec],
        out_shape=[jax.ShapeDtypeStruct((nb, S5_STATE_WIDTH), F32)] * 2,
        compiler_params=_cparams("arbitrary", "arbitrary"),
        name="s5_local",
    )(z, mats["w"])

    tn = t["tn_scan"]
    rowblk = pl.BlockSpec((tn, S5_STATE_WIDTH), lambda i: (i, 0))
    vec = pl.BlockSpec((1, S5_STATE_WIDTH), lambda i: (0, 0))
    x_re, x_im = pl.pallas_call(
        functools.partial(_s5_scan_body, tn=tn),
        grid=(nb // tn,),
        in_specs=[rowblk, rowblk, vec, vec],
        out_specs=[rowblk, rowblk],
        out_shape=[jax.ShapeDtypeStruct((nb, S5_STATE_WIDTH), F32)] * 2,
        scratch_shapes=[pltpu.VMEM((1, S5_STATE_WIDTH), F32)] * 2,
        compiler_params=_cparams("arbitrary"),
        name="s5_scan",
    )(v_re, v_im, mats["lam_re"], mats["lam_im"])

    y = pl.pallas_call(
        functools.partial(_s5_out_body, tn=tb),
        grid=(S5_N_OCT, nb // tb),
        in_specs=[u_spec, st_spec, st_spec,
                  oct_mat(LANES, (2 * S5_T - 1) * LANES), oct_mat(2 * S5_OCT_STATE, S5_OCT_IN)],
        out_specs=pl.BlockSpec((S5_T * tb, LANES), lambda o, n: (n, o)),
        out_shape=jax.ShapeDtypeStruct((seq, BRANCH_WIDTH), F32),
        scratch_shapes=[pltpu.VMEM((S5_OCT_IN, S5_OCT_IN), BF16)],
        compiler_params=_cparams("arbitrary", "arbitrary"),
        name="s5_out",
    )(z, x_re, x_im, mats["m_pad"], mats["ko"])

    tm = t["tm_glu"]
    return pl.pallas_call(
        _glu_body,
        grid=(seq // tm,),
        in_specs=[pl.BlockSpec((tm, BRANCH_WIDTH), lambda i: (i, 0)),
                  pl.BlockSpec((BRANCH_WIDTH, BRANCH_WIDTH), lambda i: (0, 0))],
        out_specs=pl.BlockSpec((tm, BRANCH_WIDTH), lambda i: (i, 0)),
        out_shape=jax.ShapeDtypeStruct((seq, BRANCH_WIDTH), BF16),
        compiler_params=_cparams("arbitrary"),
        name="s5_glu",
    )(y, w_glu)


def _merge_body(h_ref, ya_ref, yb_ref, yc_ref, yd_ref, wg_ref, wb_ref, o_ref, acc_ref):
    b = pl.program_id(1)
    y = jnp.where(b == 0, ya_ref[...],
                  jnp.where(b == 1, yb_ref[...], jnp.where(b == 2, yc_ref[...], yd_ref[...])))
    term = _sigmoid(_dot(h_ref[...], wg_ref[...])) * _dot(y, wb_ref[...])

    @pl.when(b == 0)
    def _():
        acc_ref[...] = term

    @pl.when(b > 0)
    def _():
        acc_ref[...] += term

    @pl.when(b == N_BRANCHES - 1)
    def _():
        o_ref[...] = acc_ref[...].astype(o_ref.dtype)


def _merge(h, ys, w_gate, w_branch, t):
    seq = h.shape[0]
    tm = t["tm_merge"]
    y_spec = pl.BlockSpec((tm, BRANCH_WIDTH), lambda i, b: (i, 0))
    return pl.pallas_call(
        _merge_body,
        grid=(seq // tm, N_BRANCHES),
        in_specs=[
            pl.BlockSpec((tm, D_MODEL), lambda i, b: (i, 0)),
            y_spec, y_spec, y_spec, y_spec,
            pl.BlockSpec((D_MODEL, D_MODEL), lambda i, b: (0, b)),
            pl.BlockSpec((None, BRANCH_WIDTH, D_MODEL), lambda i, b: (b, 0, 0)),
        ],
        out_specs=pl.BlockSpec((tm, D_MODEL), lambda i, b: (i, 0)),
        out_shape=jax.ShapeDtypeStruct((seq, D_MODEL), BF16),
        scratch_shapes=[pltpu.VMEM((tm, D_MODEL), F32)],
        compiler_params=_cparams("arbitrary", "arbitrary"),
        name="gated_merge",
    )(h, *ys, w_gate, w_branch)


def _out_body(m_ref, w_ref, x_ref, gpost_ref, gpre_ref, x1_ref, h2_ref):
    x1 = x_ref[...] + _rms(_dot(m_ref[...], w_ref[...]), gpost_ref[...])
    x1_ref[...] = x1
    h2_ref[...] = _rms(x1, gpre_ref[...]).astype(h2_ref.dtype)


def _out_proj(merged, w_out, x, g_post, g_pre, t):
    seq = x.shape[0]
    tm = t["tm_out"]
    row = pl.BlockSpec((tm, D_MODEL), lambda i: (i, 0))
    vec = pl.BlockSpec((1, D_MODEL), lambda i: (0, 0))
    return pl.pallas_call(
        _out_body,
        grid=(seq // tm,),
        in_specs=[row, pl.BlockSpec((D_MODEL, D_MODEL), lambda i: (0, 0)), row, vec, vec],
        out_specs=[row, row],
        out_shape=[jax.ShapeDtypeStruct((seq, D_MODEL), F32),
                   jax.ShapeDtypeStruct((seq, D_MODEL), BF16)],
        compiler_params=_cparams("arbitrary"),
        name="out_proj",
    )(merged, w_out, x, g_post, g_pre)


def _ffn_body(h_ref, wg_ref, wv_ref, cg_ref, cv_ref, wd_ref, x_ref, gp_ref, o_ref,
              acc_ref, carg_ref, carv_ref, *, tm):
    i, j = pl.program_id(0), pl.program_id(1)
    h = h_ref[...]

    def up_conv(w_ref, c_ref, car_ref):
        up = _dot(h, w_ref[...])
        prev = jnp.where(i == 0, 0.0, car_ref[j])
        car_ref[j] = up[tm - SUBLANES:tm, :]
        row = _iota(up.shape, 0)
        up1 = jnp.where(row == 0, prev[SUBLANES - 1:SUBLANES], pltpu.roll(up, 1, axis=0))
        up2 = jnp.where(row == 0, prev[SUBLANES - 2:SUBLANES - 1],
                        jnp.where(row == 1, prev[SUBLANES - 1:SUBLANES], pltpu.roll(up, 2, axis=0)))
        cw = c_ref[...]
        return cw[2:3] * up + cw[1:2] * up1 + cw[0:1] * up2

    gate = up_conv(wg_ref, cg_ref, carg_ref)
    val = up_conv(wv_ref, cv_ref, carv_ref)
    part = _dot((gate * _sigmoid(gate) * val).astype(BF16), wd_ref[...])

    @pl.when(j == 0)
    def _():
        acc_ref[...] = part

    @pl.when(j > 0)
    def _():
        acc_ref[...] += part

    @pl.when(j == pl.num_programs(1) - 1)
    def _():
        o_ref[...] = x_ref[...] + _rms(acc_ref[...], gp_ref[...])


def _ffn(h2, x1, w_gate, w_val, conv_gate, conv_val, w_down, g_post, t):
    seq = x1.shape[0]
    tm, tf = t["tm_ffn"], t["tf_ffn"]
    nf = D_FF_PAD // tf
    row = pl.BlockSpec((tm, D_MODEL), lambda i, j: (i, 0))
    return pl.pallas_call(
        functools.partial(_ffn_body, tm=tm),
        grid=(seq // tm, nf),
        in_specs=[
            row,
            pl.BlockSpec((D_MODEL, tf), lambda i, j: (0, j)),
            pl.BlockSpec((D_MODEL, tf), lambda i, j: (0, j)),
            pl.BlockSpec((FFN_CONV, tf), lambda i, j: (0, j)),
            pl.BlockSpec((FFN_CONV, tf), lambda i, j: (0, j)),
            pl.BlockSpec((tf, D_MODEL), lambda i, j: (j, 0)),
            row,
            pl.BlockSpec((1, D_MODEL), lambda i, j: (0, 0)),
        ],
        out_specs=row,
        out_shape=jax.ShapeDtypeStruct((seq, D_MODEL), F32),
        scratch_shapes=[
            pltpu.VMEM((tm, D_MODEL), F32),
            pltpu.VMEM((nf, SUBLANES, tf), F32),
            pltpu.VMEM((nf, SUBLANES, tf), F32),
        ],
        compiler_params=_cparams("arbitrary", "arbitrary"),
        name="conv_ffn",
    )(h2, w_gate, w_val, conv_gate, conv_val, w_down, x1, g_post)


W_IN_COLS = 13852
O_FF, O_GDN, O_BA, O_SU, O_DR, O_GATE = 1536, 1540, 3588, 3596, 5644, 5660


def _w_in_split_body(w_ref, mix_ref, small_ref, gate_ref):
    bf = lambda a: a.astype(BF16)
    mix_ref[:, 0:COL_GDN_QKV] = bf(w_ref[:, 0:O_FF])
    mix_ref[:, COL_GDN_QKV:COL_S5] = bf(w_ref[:, O_GDN:O_BA])
    mix_ref[:, COL_S5:MIX_WIDTH] = bf(w_ref[:, O_SU:O_DR])
    gate_ref[...] = bf(w_ref[:, O_GATE:W_IN_COLS])
    lane = _iota((w_ref.shape[0], LANES), 1)
    tile = lambda off: w_ref[:, off:off + LANES]
    t_dr, t_ff, t_ab = tile(O_DR - O_DR % LANES), tile(O_FF), tile(O_BA - O_BA % LANES)
    small = jnp.where(lane < SM_FF, pltpu.roll(t_dr, LANES - O_DR % LANES, axis=1), 0.0)
    small = small + jnp.where((lane >= SM_FF) & (lane < SM_BA), pltpu.roll(t_ff, SM_FF, axis=1), 0.0)
    small = small + jnp.where((lane >= SM_BA) & (lane < SM_BA + 2 * N_HEADS),
                              pltpu.roll(t_ab, SM_BA - O_BA % LANES, axis=1), 0.0)
    small_ref[...] = bf(small)


def _w_in_split(w_in):
    depth = w_in.shape[0]
    tr = 128
    spec = lambda width: pl.BlockSpec((None, tr, width), lambda l, i: (l, i, 0))
    widths = (MIX_WIDTH, LANES, N_BRANCHES * D_MODEL)
    return pl.pallas_call(
        _w_in_split_body,
        grid=(depth, D_MODEL // tr),
        in_specs=[spec(W_IN_COLS)],
        out_specs=[spec(w) for w in widths],
        out_shape=[jax.ShapeDtypeStruct((depth, D_MODEL, w), BF16) for w in widths],
        compiler_params=_cparams("arbitrary", "arbitrary"),
        name="w_in_split",
    )(w_in)


def _layer_weights(gla_w_gk, w_branch, w_out, ffn_w_up, ffn_conv, ffn_w_down, s5_w_glu):
    w_gk_pad = jnp.concatenate(
        [gla_w_gk, jnp.zeros((LANES - GLA_RANK, GLA_K_WIDTH), F32)], axis=0).astype(BF16)
    pad = D_FF_PAD - D_FF
    padc = lambda a: jnp.pad(a, ((0, 0), (0, pad)))
    return dict(
        w_gk_pad=w_gk_pad,
        w_branch=w_branch.astype(BF16), w_out=w_out.astype(BF16),
        w_glu=s5_w_glu.astype(BF16),
        ffn_wg=padc(ffn_w_up[:, :D_FF]).astype(BF16), ffn_wv=padc(ffn_w_up[:, D_FF:]).astype(BF16),
        ffn_cg=padc(ffn_conv[:, :D_FF]), ffn_cv=padc(ffn_conv[:, D_FF:]),
        ffn_wd=jnp.pad(ffn_w_down, ((0, pad), (0, 0))).astype(BF16),
    )


def _layer(x, p, t):
    w = _layer_weights(p["gla_w_gk"], p["w_branch"], p["w_out"], p["ffn_w_up"],
                       p["ffn_conv"], p["ffn_w_down"], p["s5_w_glu"])
    row = lambda a: a.reshape(1, -1)
    z, zs, zst, h = _in_proj(x, row(p["norm_mix_pre"]), p["w_mix"], p["w_small"], t)
    c = _forget_cumsum(zst[SM_FF:SM_FF + N_HEADS], p["fox_f_bias"])
    y_a = _fox_attention(z, c, t)
    y_b = _gdn(z, zs, zst, p["gdn_conv"], p["gdn_a_log"], p["gdn_dt_bias"], p["gdn_norm"], t)
    mats = _s5_matrices(p["s5_a_re"], p["s5_a_im"], p["s5_b_re"], p["s5_b_im"], p["s5_c_re"],
                        p["s5_c_im"], p["s5_d"], p["s5_log_dt"])
    y_c = _s5(z, mats, w["w_glu"], t)
    y_d = _gla(z, zs, w["w_gk_pad"], p["gla_b_gk"], p["gla_norm"], t)
    merged = _merge(h, (y_a, y_b, y_c, y_d), p["w_gate"], w["w_branch"], t)
    x1, h2 = _out_proj(merged, w["w_out"], x, row(p["norm_mix_post"]), row(p["norm_ffn_pre"]), t)
    return _ffn(h2, x1, w["ffn_wg"], w["ffn_wv"], w["ffn_cg"], w["ffn_cv"], w["ffn_wd"],
                row(p["norm_ffn_post"]), t)


def kernel(x, norm_mix_pre, norm_mix_post, norm_ffn_pre, norm_ffn_post, w_in, fox_f_bias, gdn_conv, gdn_a_log, gdn_dt_bias, gdn_norm, s5_a_re, s5_a_im, s5_b_re, s5_b_im, s5_c_re, s5_c_im, s5_d, s5_log_dt, s5_w_glu, gla_w_gk, gla_b_gk, gla_norm, w_branch, w_out, ffn_w_up, ffn_conv, ffn_w_down):
    params = dict(
        norm_mix_pre=norm_mix_pre, norm_mix_post=norm_mix_post, norm_ffn_pre=norm_ffn_pre,
        norm_ffn_post=norm_ffn_post, w_in=w_in, fox_f_bias=fox_f_bias, gdn_conv=gdn_conv,
        gdn_a_log=gdn_a_log, gdn_dt_bias=gdn_dt_bias, gdn_norm=gdn_norm, s5_a_re=s5_a_re,
        s5_a_im=s5_a_im, s5_b_re=s5_b_re, s5_b_im=s5_b_im, s5_c_re=s5_c_re, s5_c_im=s5_c_im,
        s5_d=s5_d, s5_log_dt=s5_log_dt, s5_w_glu=s5_w_glu, gla_w_gk=gla_w_gk, gla_b_gk=gla_b_gk,
        gla_norm=gla_norm, w_branch=w_branch, w_out=w_out, ffn_w_up=ffn_w_up, ffn_conv=ffn_conv,
        ffn_w_down=ffn_w_down)
    batch, seq, _ = x.shape
    assert batch == 1
    t = _tiles(seq)
    xs = x[0]
    del params["w_in"]
    params["w_mix"], params["w_small"], params["w_gate"] = _w_in_split(w_in)
    for layer in range(w_in.shape[0]):
        xs = _layer(xs, {k: v[layer] for k, v in params.items()}, t)
    return xs[None]
```

```python
import functools
import math

import jax
import jax.numpy as jnp
from jax import lax
from jax.experimental import pallas as pl
from jax.experimental.pallas import tpu as pltpu

F32 = jnp.float32
BF16 = jnp.bfloat16
HIGHEST = lax.Precision.HIGHEST

D_MODEL = 2048
N_HEADS = 4
HEAD_DIM = 128
BRANCH_WIDTH = 512
GDN_CONV = 4
CHUNK = 64
GDN_SOLVE = 2 * CHUNK
GLA_HEAD_K = 64
GLA_K_WIDTH = 256
GLA_RANK = 16
GLA_GATE_NORMALIZER = 16.0
GLA_SUB = 16
S5_GROUPS = 32
S5_GROUP_DIM = 16
S5_STATE = 64
S5_T = 16
D_FF = 5504
D_FF_PAD = 5632
FFN_CONV = 3
N_BRANCHES = 4
NORM_EPS = 1e-6

LANES = 128
SUBLANES = 8
VMEM_LIMIT_BYTES = 56 * 1024 * 1024

MIX_WIDTH = 5632
COL_FOX = 0
COL_GDN_QKV = 1536
COL_GDN_Z = 3072
COL_S5 = 3584
COL_GLA_Q = 4096
COL_GLA_K = 4352
COL_GLA_V = 4608
COL_GLA_G = 5120
SM_DR = 0
SM_FF = 16
SM_BA = 20
SM_BB = 24
SM_ROWS = 32


def _tiles(seq):
    return dict(
        tm_in=min(512, seq), tn_in=2816,
        tn_s5=min(256, seq // S5_T),
        t_att=min(1024, seq), strip_att=128, strip_att_t=512, tm_prep=min(512, seq),
        r_rec=min(256, seq),
        r_gdn=min(256, seq),
        tm_merge=min(512, seq),
        tm_out=min(512, seq),
        tm_ffn=min(512, seq), tf_ffn=512,
        tm_glu=min(1024, seq),
        tn_scan=min(128, seq // S5_T),
    )


def _cparams(*sem):
    return pltpu.CompilerParams(dimension_semantics=sem, vmem_limit_bytes=VMEM_LIMIT_BYTES)


def _sigmoid(x):
    return 1.0 / (1.0 + jnp.exp(-x))


def _softplus(x):
    return jnp.maximum(x, 0.0) + jnp.log1p(jnp.exp(-jnp.abs(x)))


def _log_sigmoid(x):
    return jnp.minimum(x, 0.0) - jnp.log1p(jnp.exp(-jnp.abs(x)))


def _rms(x, gain):
    return x * lax.rsqrt(jnp.mean(x * x, axis=-1, keepdims=True) + NORM_EPS) * gain


def _dot(a, b):
    return jnp.dot(a, b, preferred_element_type=F32)


def _dot_nt(a, b):
    return lax.dot_general(a, b, (((1,), (1,)), ((), ())), preferred_element_type=F32)


def _dot_tn(a, b):
    return lax.dot_general(a, b, (((0,), (0,)), ((), ())), preferred_element_type=F32)


def _split_bf16(a):
    hi = a.astype(BF16)
    lo = (a - hi.astype(F32)).astype(BF16)
    return hi, lo


def _dot3(a_split, b_split):
    ah, al = a_split
    bh, bl = b_split
    return _dot(ah, bh) + (_dot(ah, bl) + _dot(al, bh))


def _iota(shape, axis):
    return lax.broadcasted_iota(jnp.int32, shape, axis)


def _in_proj_body(x_ref, g_ref, w_ref, ws_ref, z_ref, zs_ref, zst_ref, h_ref):
    @pl.when(pl.program_id(1) == 0)
    def _():
        h = _rms(x_ref[...], g_ref[...]).astype(BF16)
        h_ref[...] = h
        zs = _dot(h, ws_ref[...])
        zs_ref[...] = zs
        zst_ref[...] = zs.T[:SM_ROWS, :]

    z_ref[...] = _dot(h_ref[...], w_ref[...])


def _in_proj(x, gain, w_mix, w_small, t):
    seq = x.shape[0]
    tm, tn = t["tm_in"], t["tn_in"]
    return pl.pallas_call(
        _in_proj_body,
        grid=(seq // tm, MIX_WIDTH // tn),
        in_specs=[
            pl.BlockSpec((tm, D_MODEL), lambda i, j: (i, 0)),
            pl.BlockSpec((1, D_MODEL), lambda i, j: (0, 0)),
            pl.BlockSpec((D_MODEL, tn), lambda i, j: (0, j)),
            pl.BlockSpec((D_MODEL, LANES), lambda i, j: (0, 0)),
        ],
        out_specs=[
            pl.BlockSpec((tm, tn), lambda i, j: (i, j)),
            pl.BlockSpec((tm, LANES), lambda i, j: (i, 0)),
            pl.BlockSpec((SM_ROWS, tm), lambda i, j: (0, i)),
            pl.BlockSpec((tm, D_MODEL), lambda i, j: (i, 0)),
        ],
        out_shape=[
            jax.ShapeDtypeStruct((seq, MIX_WIDTH), F32),
            jax.ShapeDtypeStruct((seq, LANES), F32),
            jax.ShapeDtypeStruct((SM_ROWS, seq), F32),
            jax.ShapeDtypeStruct((seq, D_MODEL), BF16),
        ],
        compiler_params=_cparams("arbitrary", "arbitrary"),
        name="in_proj",
    )(x, gain, w_mix, w_small)


def _cum_body(ff_ref, b_ref, c_ref, *, rows):
    lf = _log_sigmoid(ff_ref[...] + b_ref[...]).reshape(N_HEADS * rows, LANES)
    upper = (_iota((LANES, LANES), 0) <= _iota((LANES, LANES), 1)).astype(F32)
    loc = jnp.dot(lf, upper, precision=HIGHEST, preferred_element_type=F32)
    tot = jnp.broadcast_to(loc[:, LANES - 1:LANES], loc.shape)
    n = N_HEADS * rows
    rr, cc = _iota((n, n), 0), _iota((n, n), 1)
    shift = int(math.log2(rows))
    same_head = lax.shift_right_logical(rr, shift) == lax.shift_right_logical(cc, shift)
    low = ((cc < rr) & same_head).astype(F32)
    off = jnp.dot(low, tot, precision=HIGHEST, preferred_element_type=F32)
    c_ref[...] = (loc + off).reshape(N_HEADS, rows, LANES)


def _forget_cumsum(ff_t, bias):
    seq = ff_t.shape[1]
    rows = seq // LANES
    assert rows & (rows - 1) == 0 and rows % SUBLANES == 0
    c = pl.pallas_call(
        functools.partial(_cum_body, rows=rows),
        out_shape=jax.ShapeDtypeStruct((N_HEADS, rows, LANES), F32),
        compiler_params=pltpu.CompilerParams(vmem_limit_bytes=VMEM_LIMIT_BYTES),
        name="fox_cumsum",
    )(ff_t.reshape(N_HEADS, rows, LANES), bias.reshape(N_HEADS, 1, 1))
    return c.reshape(N_HEADS, seq)


def _fox_body(qi_ref, kj_ref, q_ref, k_ref, v_ref, cq_ref, ck_ref, o_ref,
              m_ref, l_ref, acc_ref, qs_ref, *, strip):
    p = pl.program_id(1)
    qi, kj = qi_ref[p], kj_ref[p]

    @pl.when(kj == 0)
    def _():
        m_ref[...] = jnp.full(m_ref.shape, -jnp.inf, F32)
        l_ref[...] = jnp.zeros(l_ref.shape, F32)
        acc_ref[...] = jnp.zeros(acc_ref.shape, F32)
        qs_ref[...] = (q_ref[...] * (HEAD_DIM ** -0.5)).astype(BF16)

    def step(diagonal):
        kb = k_ref[...].astype(BF16)
        vb = v_ref[...].astype(BF16)
        bias = cq_ref[0, :, 0:1] - ck_ref[0]
        tq = qs_ref.shape[0]
        n_strips = tq // strip
        n_keys = lambda r: (r + 1) * strip if diagonal else kb.shape[0]

        def scores(r):
            nk = n_keys(r)
            s = _dot_nt(qs_ref[r * strip:(r + 1) * strip, :], kb[:nk]) + bias[:, :nk]
            if diagonal:
                s = jnp.where(_iota(s.shape, 1) <= _iota(s.shape, 0) + r * strip, s, -jnp.inf)
            return s

        s_next = scores(0)
        for r in range(n_strips):
            rows = slice(r * strip, (r + 1) * strip)
            nk = n_keys(r)
            s = s_next
            if r + 1 < n_strips:
                s_next = scores(r + 1)
            m_prev = m_ref[rows, :]
            m_new = jnp.maximum(m_prev, jnp.max(s, axis=-1, keepdims=True))
            alpha = jnp.exp(m_prev - m_new)
            e = jnp.exp(s - m_new)
            l_ref[rows, :] = alpha * l_ref[rows, :] + jnp.sum(e, axis=-1, keepdims=True)
            acc_ref[rows, :] = alpha * acc_ref[rows, :] + _dot(e.astype(BF16), vb[:nk])
            m_ref[rows, :] = m_new

    @pl.when(kj < qi)
    def _():
        step(False)

    @pl.when(kj == qi)
    def _():
        step(True)
        o_ref[...] = (acc_ref[...] / l_ref[...]).astype(o_ref.dtype)


def _fox_attention(z, c, t):
    seq = z.shape[0]
    tq = t["t_att"]
    nq = seq // tq
    pairs = [(i, j) for i in range(nq) for j in range(i + 1)]
    qi = jnp.asarray([a for a, _ in pairs], jnp.int32)
    kj = jnp.asarray([b for _, b in pairs], jnp.int32)
    cb = COL_FOX // HEAD_DIM
    grid_spec = pltpu.PrefetchScalarGridSpec(
        num_scalar_prefetch=2,
        grid=(N_HEADS, len(pairs)),
        in_specs=[
            pl.BlockSpec((tq, HEAD_DIM), lambda h, p, qi, kj: (qi[p], cb + h)),
            pl.BlockSpec((tq, HEAD_DIM), lambda h, p, qi, kj: (kj[p], cb + N_HEADS + h)),
            pl.BlockSpec((tq, HEAD_DIM), lambda h, p, qi, kj: (kj[p], cb + 2 * N_HEADS + h)),
            pl.BlockSpec((1, 1, tq), lambda h, p, qi, kj: (h, 0, qi[p])),
            pl.BlockSpec((1, 1, tq), lambda h, p, qi, kj: (h, 0, kj[p])),
        ],
        out_specs=pl.BlockSpec((tq, HEAD_DIM), lambda h, p, qi, kj: (qi[p], h)),
        scratch_shapes=[
            pltpu.VMEM((tq, 1), F32),
            pltpu.VMEM((tq, 1), F32),
            pltpu.VMEM((tq, HEAD_DIM), F32),
            pltpu.VMEM((tq, HEAD_DIM), BF16),
        ],
    )
    c3 = c.reshape(N_HEADS, 1, seq)
    return pl.pallas_call(
        functools.partial(_fox_body, strip=min(t["strip_att"], tq)),
        grid_spec=grid_spec,
        out_shape=jax.ShapeDtypeStruct((seq, BRANCH_WIDTH), BF16),
        compiler_params=_cparams("arbitrary", "arbitrary"),
        name="fox_attention",
    )(qi, kj, z, z, z, c3, c3)


FOX_AUG = 2 * HEAD_DIM
LOG2E = 1.4426950408889634


def _fox_prep_body(z_ref, zs_ref, fb_ref, k_ref, qt_ref, vt_ref, carry_ref, *, tm):
    @pl.when(pl.program_id(0) == 0)
    def _():
        carry_ref[...] = jnp.zeros(carry_ref.shape, F32)

    lf = _log_sigmoid(zs_ref[...] + fb_ref[...])
    tri = (_iota((tm, tm), 1) <= _iota((tm, tm), 0)).astype(F32)
    c_col = jnp.dot(tri, lf, precision=HIGHEST, preferred_element_type=F32) + carry_ref[0:1, :]
    carry_ref[0:1, :] = c_col[tm - 1:tm, :]
    lane = _iota((tm, HEAD_DIM), 1)
    ones_t = jnp.where(_iota((HEAD_DIM, tm), 0) < 3, 1.0, 0.0).astype(BF16)
    for h in range(N_HEADS):
        col = lambda base: slice(base + h * HEAD_DIM, base + (h + 1) * HEAD_DIM)
        b = c_col[:, SM_FF + h:SM_FF + h + 1] * (-LOG2E)
        hi = b.astype(BF16).astype(F32)
        mid = (b - hi).astype(BF16).astype(F32)
        lo = b - hi - mid
        aug = jnp.where(lane == 0, hi, jnp.where(lane == 1, mid, jnp.where(lane == 2, lo, 0.0)))
        k_ref[h, :, 0:HEAD_DIM] = z_ref[:, col(BRANCH_WIDTH)].astype(BF16)
        k_ref[h, :, HEAD_DIM:FOX_AUG] = aug.astype(BF16)
        q = (z_ref[:, col(0)] * (HEAD_DIM ** -0.5 * LOG2E)).astype(BF16)
        qt_ref[h, 0:HEAD_DIM, :] = q.astype(F32).T.astype(BF16)
        qt_ref[h, HEAD_DIM:FOX_AUG, :] = ones_t
        vt_ref[h] = z_ref[:, col(2 * BRANCH_WIDTH)].T.astype(BF16)


def _fox_prep(z, zs, f_bias, t):
    seq = z.shape[0]
    tm = t["tm_prep"]
    fb = jnp.zeros((1, LANES), F32).at[0, SM_FF:SM_FF + N_HEADS].set(f_bias)
    return pl.pallas_call(
        functools.partial(_fox_prep_body, tm=tm),
        grid=(seq // tm,),
        in_specs=[pl.BlockSpec((tm, 3 * BRANCH_WIDTH), lambda i: (i, COL_FOX // (3 * BRANCH_WIDTH))),
                  pl.BlockSpec((tm, LANES), lambda i: (i, 0)),
                  pl.BlockSpec((1, LANES), lambda i: (0, 0))],
        out_specs=[pl.BlockSpec((N_HEADS, tm, FOX_AUG), lambda i: (0, i, 0)),
                   pl.BlockSpec((N_HEADS, FOX_AUG, tm), lambda i: (0, 0, i)),
                   pl.BlockSpec((N_HEADS, HEAD_DIM, tm), lambda i: (0, 0, i))],
        out_shape=[jax.ShapeDtypeStruct((N_HEADS, seq, FOX_AUG), BF16),
                   jax.ShapeDtypeStruct((N_HEADS, FOX_AUG, seq), BF16),
                   jax.ShapeDtypeStruct((N_HEADS, HEAD_DIM, seq), BF16)],
        scratch_shapes=[pltpu.VMEM((SUBLANES, LANES), F32)],
        compiler_params=_cparams("arbitrary"),
        name="fox_prep",
    )(z, zs, fb)


def _fox_t_body(qi_ref, kj_ref, k_ref, qt_ref, vt_ref, o_ref, m_ref, l_ref, acc_ref, *, strip):
    p = pl.program_id(1)
    qi, kj = qi_ref[p], kj_ref[p]
    tq = qt_ref.shape[1]

    @pl.when(kj == 0)
    def _():
        m_ref[...] = jnp.full(m_ref.shape, -jnp.inf, F32)
        l_ref[...] = jnp.zeros(l_ref.shape, F32)
        acc_ref[...] = jnp.zeros(acc_ref.shape, F32)

    def step(diagonal):
        n_strips = tq // strip
        n_keys = lambda r: (r + 1) * strip if diagonal else k_ref.shape[0]

        def scores(r):
            s = _dot(k_ref[0:n_keys(r), :], qt_ref[:, r * strip:(r + 1) * strip])
            if diagonal:
                s = jnp.where(_iota(s.shape, 0) <= _iota(s.shape, 1) + r * strip, s, -jnp.inf)
            return s

        s_next = scores(0)
        for r in range(n_strips):
            cols = slice(r * strip, (r + 1) * strip)
            s = s_next
            if r + 1 < n_strips:
                s_next = scores(r + 1)
            m_prev = m_ref[0:1, cols]
            m_new = jnp.maximum(m_prev, jnp.max(s, axis=0, keepdims=True))
            alpha = jnp.exp2(m_prev - m_new)
            e = jnp.exp2(s - m_new)
            l_ref[0:1, cols] = alpha * l_ref[0:1, cols] + jnp.sum(e, axis=0, keepdims=True)
            acc_ref[:, cols] = alpha * acc_ref[:, cols] + _dot(vt_ref[:, 0:n_keys(r)], e.astype(BF16))
            m_ref[0:1, cols] = m_new

    @pl.when(kj < qi)
    def _():
        step(False)

    @pl.when(kj == qi)
    def _():
        step(True)
        o_ref[...] = (acc_ref[...] / l_ref[0:1, :]).T.astype(o_ref.dtype)


def _fox_attention_t(k_aug, q_t, v_t, t):
    seq = k_aug.shape[1]
    tq = t["t_att"]
    nq = seq // tq
    pairs = [(i, j) for i in range(nq) for j in range(i + 1)]
    qi = jnp.asarray([a for a, _ in pairs], jnp.int32)
    kj = jnp.asarray([b for _, b in pairs], jnp.int32)
    grid_spec = pltpu.PrefetchScalarGridSpec(
        num_scalar_prefetch=2,
        grid=(N_HEADS, len(pairs)),
        in_specs=[
            pl.BlockSpec((None, tq, FOX_AUG), lambda h, p, qi, kj: (h, kj[p], 0)),
            pl.BlockSpec((None, FOX_AUG, tq), lambda h, p, qi, kj: (h, 0, qi[p])),
            pl.BlockSpec((None, HEAD_DIM, tq), lambda h, p, qi, kj: (h, 0, kj[p])),
        ],
        out_specs=pl.BlockSpec((tq, HEAD_DIM), lambda h, p, qi, kj: (qi[p], h)),
        scratch_shapes=[
            pltpu.VMEM((SUBLANES, tq), F32),
            pltpu.VMEM((SUBLANES, tq), F32),
            pltpu.VMEM((HEAD_DIM, tq), F32),
        ],
    )
    return pl.pallas_call(
        functools.partial(_fox_t_body, strip=min(t["strip_att_t"], tq)),
        grid_spec=grid_spec,
        out_shape=jax.ShapeDtypeStruct((seq, BRANCH_WIDTH), BF16),
        compiler_params=_cparams("arbitrary", "arbitrary"),
        name="fox_attention_t",
    )(qi, kj, k_aug, q_t, v_t)


def _unit_lower_inverses(mats):
    n = mats[0].shape[0]
    eye = (_iota((n, n), 0) == _iota((n, n), 1)).astype(F32)
    ps = [eye - a for a in mats]
    ms = [_split_bf16(a) for a in mats]
    for _ in range(int(math.log2(CHUNK)) - 1):
        ms = [_split_bf16(_dot3(m, m)) for m in ms]
        ps = [p + _dot3(_split_bf16(p), m) for p, m in zip(ps, ms)]
    return ps


def _gdn_body(qkv_ref, bz_ref, zs_ref, zst_ref, cw_ref, alr_ref, alc_ref, dtr_ref, dtc_ref,
              gn_ref, o_ref, ext_ref, s_ref, *, rows):
    @pl.when(pl.program_id(0) == 0)
    def _():
        ext_ref[0:SUBLANES, :] = jnp.zeros((SUBLANES, ext_ref.shape[1]), F32)
        s_ref[...] = jnp.zeros(s_ref.shape, F32)

    x = qkv_ref[...]
    ext_ref[SUBLANES:SUBLANES + rows, :] = x
    cw = cw_ref[...]
    conv = cw[GDN_CONV - 1:GDN_CONV] * x
    for j in range(GDN_CONV - 1):
        off = SUBLANES - (GDN_CONV - 1) + j
        conv = conv + cw[j:j + 1] * ext_ref[off:off + rows, :]
    ext_ref[0:SUBLANES, :] = x[rows - SUBLANES:rows, :]
    qkv = conv * _sigmoid(conv)

    zs = zs_ref[...]
    zst = zst_ref[...]
    g_col = -jnp.exp(alr_ref[...]) * _softplus(zs[:, SM_BA:SM_BA + N_HEADS] + dtr_ref[...])
    g_row = -jnp.exp(alc_ref[...]) * _softplus(zst[SM_BA:SM_BA + N_HEADS, :] + dtc_ref[...])
    beta = _sigmoid(zs[:, SM_BB:SM_BB + N_HEADS])
    rr, cc = _iota((rows, rows), 0), _iota((rows, rows), 1)
    shift = int(math.log2(CHUNK))
    same = lax.shift_right_logical(rr, shift) == lax.shift_right_logical(cc, shift)
    incl = (same & (cc <= rr)).astype(F32)
    gc_col = jnp.dot(incl, g_col, precision=HIGHEST, preferred_element_type=F32)
    gc_row = _dot_nt_highest(g_row, incl)
    strict = (same & (cc < rr))[:GDN_SOLVE, :GDN_SOLVE]
    n_chunks = rows // CHUNK
    rc, ccn = _iota((CHUNK, CHUNK), 0), _iota((CHUNK, CHUNK), 1)
    causal = ccn <= rc

    heads = range(N_HEADS)
    blocks = [slice(b * GDN_SOLVE, (b + 1) * GDN_SOLVE) for b in range(rows // GDN_SOLVE)]
    hs = lambda base, h: slice(base + h * HEAD_DIM, base + (h + 1) * HEAD_DIM)
    qs, ks, kbs, gccs, gcrs, rhss, a_kks = [], [], [], [], [], [], []
    for h in heads:
        q, k, v = qkv[:, hs(0, h)], qkv[:, hs(BRANCH_WIDTH, h)], qkv[:, hs(2 * BRANCH_WIDTH, h)]
        q = q * lax.rsqrt(jnp.sum(q * q, axis=-1, keepdims=True) + NORM_EPS) * (HEAD_DIM ** -0.5)
        k = k * lax.rsqrt(jnp.sum(k * k, axis=-1, keepdims=True) + NORM_EPS)
        gcc = gc_col[:, h:h + 1]
        gcr = gc_row[h:h + 1, :]
        bt = beta[:, h:h + 1]
        kb = k.astype(BF16)
        gam = jnp.exp(gcc)
        qs.append((q, q * gam))
        ks.append(k)
        kbs.append(kb)
        gccs.append(gcc)
        gcrs.append(gcr)
        rhss.append(jnp.concatenate([(bt * gam) * k, bt * v], axis=-1))
        for sl in blocks:
            decay = jnp.exp(jnp.minimum(gcc[sl] - gcr[:, sl], 0.0))
            a_kks.append(jnp.where(strict, bt[sl] * _dot_nt(kb[sl], kb[sl]) * decay, 0.0))
    tinvs = _unit_lower_inverses(a_kks)
    sols = []
    for h in heads:
        parts = [_dot3(_split_bf16(tinvs[h * len(blocks) + b]), _split_bf16(rhss[h][sl]))
                 for b, sl in enumerate(blocks)]
        sols.append(jnp.concatenate(parts, axis=0) if len(parts) > 1 else parts[0])
    states = [s_ref[h] for h in heads]
    outs = [[] for _ in heads]
    for c in range(n_chunks):
        lo, hi = c * CHUNK, (c + 1) * CHUNK
        for h in heads:
            (q, q_dec), k, kb, gcc, gcr = qs[h], ks[h], kbs[h], gccs[h], gcrs[h]
            w_mat, u_mat = sols[h][lo:hi, :HEAD_DIM], sols[h][lo:hi, HEAD_DIM:]
            g_last = gcc[hi - 1:hi, :]
            decay = jnp.exp(jnp.minimum(gcc[lo:hi] - gcr[:, lo:hi], 0.0))
            a_qk = jnp.where(causal, _dot_nt(q[lo:hi].astype(BF16), kb[lo:hi]) * decay, 0.0)
            k_dec = k[lo:hi] * jnp.exp(g_last - gcc[lo:hi])
            sb = states[h].astype(BF16)
            delta = u_mat - _dot(w_mat.astype(BF16), sb)
            db = delta.astype(BF16)
            outs[h].append(_dot(q_dec[lo:hi].astype(BF16), sb) + _dot(a_qk.astype(BF16), db))
            states[h] = jnp.exp(g_last) * states[h] + _dot_tn(k_dec.astype(BF16), db)
    for h in heads:
        s_ref[h] = states[h]
        o = jnp.concatenate(outs[h], axis=0) if n_chunks > 1 else outs[h][0]
        zg = bz_ref[:, h * HEAD_DIM:(h + 1) * HEAD_DIM]
        o_ref[:, h * HEAD_DIM:(h + 1) * HEAD_DIM] = (
            _rms(o, gn_ref[...]) * (zg * _sigmoid(zg))).astype(o_ref.dtype)


def _dot_nt_highest(a, b):
    return lax.dot_general(a, b, (((1,), (1,)), ((), ())), precision=HIGHEST,
                           preferred_element_type=F32)


def _gdn(z, zs, zst, conv_w, a_log, dt_bias, norm_gain, t):
    seq = z.shape[0]
    rows = t["r_gdn"]
    assert rows % GDN_SOLVE == 0
    qkv_w = 3 * BRANCH_WIDTH
    full = lambda shape: pl.BlockSpec(shape, lambda i: (0,) * len(shape))
    return pl.pallas_call(
        functools.partial(_gdn_body, rows=rows),
        grid=(seq // rows,),
        in_specs=[
            pl.BlockSpec((rows, qkv_w), lambda i: (i, COL_GDN_QKV // qkv_w)),
            pl.BlockSpec((rows, BRANCH_WIDTH), lambda i: (i, COL_GDN_Z // BRANCH_WIDTH)),
            pl.BlockSpec((rows, LANES), lambda i: (i, 0)),
            pl.BlockSpec((SM_ROWS, rows), lambda i: (0, i)),
            full((GDN_CONV, qkv_w)),
            full((1, N_HEADS)), full((N_HEADS, 1)), full((1, N_HEADS)), full((N_HEADS, 1)),
            full((1, HEAD_DIM)),
        ],
        out_specs=pl.BlockSpec((rows, BRANCH_WIDTH), lambda i: (i, 0)),
        out_shape=jax.ShapeDtypeStruct((seq, BRANCH_WIDTH), BF16),
        scratch_shapes=[
            pltpu.VMEM((rows + SUBLANES, qkv_w), F32),
            pltpu.VMEM((N_HEADS, HEAD_DIM, HEAD_DIM), F32),
        ],
        compiler_params=_cparams("arbitrary"),
        name="gated_deltanet",
    )(z, z, zs, zst, conv_w, a_log.reshape(1, -1), a_log.reshape(-1, 1),
      dt_bias.reshape(1, -1), dt_bias.reshape(-1, 1), norm_gain.reshape(1, -1))


def _gla_body(q_ref, k_ref, v_ref, g_ref, zs_ref, wgk_ref, bgk_ref, gn_ref, o_ref, s_ref, *, rows):
    @pl.when(pl.program_id(0) == 0)
    def _():
        s_ref[...] = jnp.zeros(s_ref.shape, F32)

    lg = _log_sigmoid(_dot(zs_ref[...].astype(BF16), wgk_ref[...]) + bgk_ref[...])
    lg = lg * (1.0 / GLA_GATE_NORMALIZER)
    rr, cc = _iota((rows, rows), 0), _iota((rows, rows), 1)
    shift = int(math.log2(CHUNK))
    same = lax.shift_right_logical(rr, shift) == lax.shift_right_logical(cc, shift)
    incl = (same & (cc <= rr)).astype(F32)
    gcs = jnp.dot(incl, lg, precision=HIGHEST, preferred_element_type=F32)
    rc, ccn = _iota((CHUNK, CHUNK), 0), _iota((CHUNK, CHUNK), 1)
    causal = ccn <= rc
    n_chunks = rows // CHUNK
    qa = q_ref[...] * (GLA_HEAD_K ** -0.5)
    ka = k_ref[...]

    heads = range(N_HEADS)
    k_slice = lambda h: slice(h * GLA_HEAD_K, (h + 1) * GLA_HEAD_K)
    v_slice = lambda h: slice(h * HEAD_DIM, (h + 1) * HEAD_DIM)
    pairs = [(c, h) for c in range(n_chunks) for h in heads]
    a_mats, kvs, q_decs, vbs, g_lasts = {}, {}, {}, {}, {}
    for c, h in pairs:
        lo, hi = c * CHUNK, (c + 1) * CHUNK
        q, k, g = qa[lo:hi, k_slice(h)], ka[lo:hi, k_slice(h)], gcs[lo:hi, k_slice(h)]
        v = v_ref[lo:hi, v_slice(h)].astype(BF16)
        g_last = g[CHUNK - 1:CHUNK, :]
        blocks = []
        for sblk in range(CHUNK // GLA_SUB):
            a, b = sblk * GLA_SUB, (sblk + 1) * GLA_SUB
            ref = g[a:a + 1, :]
            qt = q[a:b] * jnp.exp(g[a:b] - ref)
            kt = k * jnp.exp(jnp.minimum(ref - g, 80.0))
            blocks.append(_dot_nt(qt.astype(BF16), kt.astype(BF16)))
        a_mats[c, h] = jnp.where(causal, jnp.concatenate(blocks, axis=0), 0.0).astype(BF16)
        kvs[c, h] = _dot_tn(v, (k * jnp.exp(g_last - g)).astype(BF16))
        q_decs[c, h] = (q * jnp.exp(g)).astype(BF16)
        vbs[c, h], g_lasts[c, h] = v, g_last
    states = [s_ref[h] for h in heads]
    outs = [[] for _ in heads]
    for c, h in pairs:
        inter = _dot_nt(q_decs[c, h], states[h].astype(BF16))
        outs[h].append(inter + _dot(a_mats[c, h], vbs[c, h]))
        states[h] = jnp.exp(g_lasts[c, h]) * states[h] + kvs[c, h]
    for h in heads:
        vs = v_slice(h)
        s_ref[h] = states[h]
        o = jnp.concatenate(outs[h], axis=0) if n_chunks > 1 else outs[h][0]
        gate = g_ref[:, vs]
        o_ref[:, vs] = (_rms(o, gn_ref[...]) * (gate * _sigmoid(gate))).astype(o_ref.dtype)


def _gla(z, zs, w_gk_pad, b_gk, norm_gain, t):
    seq = z.shape[0]
    rows = t["r_rec"]
    full = lambda shape: pl.BlockSpec(shape, lambda i: (0,) * len(shape))
    return pl.pallas_call(
        functools.partial(_gla_body, rows=rows),
        grid=(seq // rows,),
        in_specs=[
            pl.BlockSpec((rows, GLA_K_WIDTH), lambda i: (i, COL_GLA_Q // GLA_K_WIDTH)),
            pl.BlockSpec((rows, GLA_K_WIDTH), lambda i: (i, COL_GLA_K // GLA_K_WIDTH)),
            pl.BlockSpec((rows, BRANCH_WIDTH), lambda i: (i, COL_GLA_V // BRANCH_WIDTH)),
            pl.BlockSpec((rows, BRANCH_WIDTH), lambda i: (i, COL_GLA_G // BRANCH_WIDTH)),
            pl.BlockSpec((rows, LANES), lambda i: (i, 0)),
            full((LANES, GLA_K_WIDTH)), full((1, GLA_K_WIDTH)), full((1, HEAD_DIM)),
        ],
        out_specs=pl.BlockSpec((rows, BRANCH_WIDTH), lambda i: (i, 0)),
        out_shape=jax.ShapeDtypeStruct((seq, BRANCH_WIDTH), BF16),
        scratch_shapes=[pltpu.VMEM((N_HEADS, HEAD_DIM, GLA_HEAD_K), F32)],
        compiler_params=_cparams("arbitrary"),
        name="gla",
    )(z, z, z, z, zs, w_gk_pad, b_gk.reshape(1, -1), norm_gain.reshape(1, -1))


S5_OCT = LANES // S5_GROUP_DIM
S5_N_OCT = S5_GROUPS // S5_OCT
S5_OCT_IN = S5_T * LANES
S5_OCT_STATE = S5_OCT * S5_STATE
S5_STATE_WIDTH = S5_GROUPS * S5_STATE


def _s5_matrices(a_re, a_im, b_re, b_im, c_re, c_im, d, log_dt):
    g, p, hg, tt, no, oc = S5_GROUPS, S5_STATE, S5_GROUP_DIM, S5_T, S5_N_OCT, S5_OCT
    lam = lax.complex(a_re.astype(F32), a_im.astype(F32))
    ldt = lam * jnp.exp(log_dt.astype(F32))[:, None]
    lam_bar = jnp.exp(ldt)
    b_bar = ((lam_bar - 1.0) / lam)[..., None] * lax.complex(b_re.astype(F32), b_im.astype(F32))
    c = lax.complex(c_re.astype(F32), c_im.astype(F32))
    steps = jnp.arange(tt + 1, dtype=F32)
    pw = jnp.exp(ldt[None] * steps[:, None, None])
    k_lag = jnp.einsum("ghp,lgp,gpk->gklh", c, pw[:tt], b_bar, precision=HIGHEST).real
    k_lag = k_lag.at[:, :, 0, :].add(jnp.eye(hg, dtype=F32)[None] * d.astype(F32).reshape(g, 1, hg))
    kc = k_lag.reshape(no, LANES, tt * hg)
    tau = jnp.arange(tt)
    w = pw[tt - 1 - tau][:, :, None, :] * b_bar.transpose(0, 2, 1)[None]
    w = jnp.stack([w.real, w.imag], axis=3)
    wc = w.reshape(tt, no, oc, hg, 2 * p).transpose(1, 0, 2, 3, 4).reshape(no, S5_OCT_IN, 2 * p)
    ko = c.transpose(0, 2, 1)[:, :, None, :] * pw[1:tt + 1].transpose(1, 2, 0)[:, :, :, None]
    ko = jnp.stack([ko.real, -ko.imag], axis=0)
    koc = ko.reshape(2, no, oc * p, tt * hg).transpose(1, 0, 2, 3).reshape(no, 2 * S5_OCT_STATE, tt * hg)
    lam_t = pw[tt].reshape(1, g * p)
    m_pad, w_cat, ko_cat = _s5_expand(kc, wc, koc)
    return dict(m_pad=m_pad, w=w_cat, ko=ko_cat, lam_re=lam_t.real, lam_im=lam_t.imag)


def _s5_expand_body(kc_ref, wc_ref, koc_ref, mpad_ref, w_ref, ko_ref):
    hg, p = S5_GROUP_DIM, S5_STATE
    sh = lambda x, n: lax.shift_right_logical(x, int(math.log2(n)))
    md = lambda x, n: jnp.bitwise_and(x, n - 1)

    def replicate(src, out_cols, unit):
        n_in = src.shape[1]
        a, c = _iota((n_in, out_cols), 0), _iota((n_in, out_cols), 1)
        rep = (sh(a, unit) == sh(c, unit * S5_OCT)) & (md(a, unit) == md(c, unit))
        return _dot(src.astype(BF16), rep.astype(BF16))

    def keep_diagonal(x, row_unit, col_unit):
        r, c = _iota(x.shape, 0), _iota(x.shape, 1)
        same = md(sh(r, row_unit), S5_OCT) == md(sh(c, col_unit), S5_OCT)
        return jnp.where(same, x, 0.0).astype(BF16)

    m = keep_diagonal(replicate(kc_ref[...], S5_OCT_IN, hg), hg, hg)
    mpad_ref[:, 0:(S5_T - 1) * LANES] = jnp.zeros((LANES, (S5_T - 1) * LANES), BF16)
    mpad_ref[:, (S5_T - 1) * LANES:] = m
    w_ref[...] = keep_diagonal(replicate(wc_ref[...], 2 * S5_OCT_STATE, p), hg, p)
    ko_ref[...] = keep_diagonal(replicate(koc_ref[...], S5_OCT_IN, hg), p, hg)


def _s5_expand(kc, wc, koc):
    no = S5_N_OCT
    shapes = ((LANES, (2 * S5_T - 1) * LANES), (S5_OCT_IN, 2 * S5_OCT_STATE),
              (2 * S5_OCT_STATE, S5_OCT_IN))
    spec = lambda a, b: pl.BlockSpec((None, a, b), lambda o: (o, 0, 0))
    return pl.pallas_call(
        _s5_expand_body,
        grid=(no,),
        in_specs=[spec(*kc.shape[1:]), spec(*wc.shape[1:]), spec(*koc.shape[1:])],
        out_specs=[spec(*s) for s in shapes],
        out_shape=[jax.ShapeDtypeStruct((no,) + s, BF16) for s in shapes],
        compiler_params=_cparams("arbitrary"),
        name="s5_expand",
    )(kc, wc, koc)


def _s5_gather(u_ref, tn):
    return jnp.concatenate(
        [u_ref[pl.ds(s, tn, stride=S5_T), :] for s in range(S5_T)], axis=1).astype(BF16)


def _s5_local_body(u_ref, w_ref, vre_ref, vim_ref, *, tn):
    v = _dot(_s5_gather(u_ref, tn), w_ref[...])
    vre_ref[...] = v[:, :S5_OCT_STATE]
    vim_ref[...] = v[:, S5_OCT_STATE:]


def _s5_scan_body(vre_ref, vim_ref, lre_ref, lim_ref, xre_ref, xim_ref, sre_ref, sim_ref, *, tn):
    @pl.when(pl.program_id(0) == 0)
    def _():
        sre_ref[...] = jnp.zeros(sre_ref.shape, F32)
        sim_ref[...] = jnp.zeros(sim_ref.shape, F32)

    lre, lim = lre_ref[...], lim_ref[...]

    def body(m, carry):
        xr, xi = carry
        base = pl.multiple_of(m * SUBLANES, SUBLANES)
        vr = vre_ref[pl.ds(base, SUBLANES), :]
        vi = vim_ref[pl.ds(base, SUBLANES), :]
        out_r, out_i = [], []
        for j in range(SUBLANES):
            out_r.append(xr)
            out_i.append(xi)
            xr, xi = (lre * xr - lim * xi + vr[j:j + 1], lre * xi + lim * xr + vi[j:j + 1])
        xre_ref[pl.ds(base, SUBLANES), :] = jnp.concatenate(out_r, axis=0)
        xim_ref[pl.ds(base, SUBLANES), :] = jnp.concatenate(out_i, axis=0)
        return xr, xi

    xr, xi = lax.fori_loop(0, tn // SUBLANES, body, (sre_ref[...], sim_ref[...]))
    sre_ref[...] = xr
    sim_ref[...] = xi


def _gelu_tanh(x):
    return 0.5 * x * (1.0 + jnp.tanh(math.sqrt(2.0 / math.pi) * (x + 0.044715 * (x * x * x))))


def _s5_out_body(u_ref, xre_ref, xim_ref, mpad_ref, ko_ref, y_ref, ki_ref, *, tn):
    @pl.when(pl.program_id(1) == 0)
    def _():
        for s in range(S5_T):
            off = (S5_T - 1 - s) * LANES
            ki_ref[s * LANES:(s + 1) * LANES, :] = mpad_ref[:, off:off + S5_OCT_IN]

    x = jnp.concatenate([xre_ref[...], xim_ref[...]], axis=1).astype(BF16)
    y = _gelu_tanh(_dot(_s5_gather(u_ref, tn), ki_ref[...]) + _dot(x, ko_ref[...]))
    for s in range(S5_T):
        y_ref[pl.ds(s, tn, stride=S5_T), :] = y[:, s * LANES:(s + 1) * LANES]


def _glu_body(y_ref, w_ref, o_ref):
    y = y_ref[...]
    o_ref[...] = (y * _sigmoid(_dot(y.astype(BF16), w_ref[...]))).astype(o_ref.dtype)


def _s5(z, mats, w_glu, t):
    seq = z.shape[0]
    nb = seq // S5_T
    tb = t["tn_s5"]
    col0 = COL_S5 // LANES
    u_spec = pl.BlockSpec((S5_T * tb, LANES), lambda o, n: (n, col0 + o))
    st_spec = pl.BlockSpec((tb, S5_OCT_STATE), lambda o, n: (n, o))
    oct_mat = lambda a, b: pl.BlockSpec((None, a, b), lambda o, n: (o, 0, 0))
    v_re, v_im = pl.pallas_call(
        functools.partial(_s5_local_body, tn=tb),
        grid=(S5_N_OCT, nb // tb),
        in_specs=[u_spec, oct_mat(S5_OCT_IN, 2 * S5_OCT_STATE)],
        out_specs=[st_spec, st_spec],
        out_shape=[jax.ShapeDtypeStruct((nb, S5_STATE_WIDTH), F32)] * 2,
        compiler_params=_cparams("arbitrary", "arbitrary"),
        name="s5_local",
    )(z, mats["w"])

    tn = t["tn_scan"]
    rowblk = pl.BlockSpec((tn, S5_STATE_WIDTH), lambda i: (i, 0))
    vec = pl.BlockSpec((1, S5_STATE_WIDTH), lambda i: (0, 0))
    x_re, x_im = pl.pallas_call(
        functools.partial(_s5_scan_body, tn=tn),
        grid=(nb // tn,),
        in_specs=[rowblk, rowblk, vec, vec],
        out_specs=[rowblk, rowblk],
        out_shape=[jax.ShapeDtypeStruct((nb, S5_STATE_WIDTH), F32)] * 2,
        scratch_shapes=[pltpu.VMEM((1, S5_STATE_WIDTH), F32)] * 2,
        compiler_params=_cparams("arbitrary"),
        name="s5_scan",
    )(v_re, v_im, mats["lam_re"], mats["lam_im"])

    y = pl.pallas_call(
        functools.partial(_s5_out_body, tn=tb),
        grid=(S5_N_OCT, nb // tb),
        in_specs=[u_spec, st_spec, st_spec,
                  oct_mat(LANES, (2 * S5_T - 1) * LANES), oct_mat(2 * S5_OCT_STATE, S5_OCT_IN)],
        out_specs=pl.BlockSpec((S5_T * tb, LANES), lambda o, n: (n, o)),
        out_shape=jax.ShapeDtypeStruct((seq, BRANCH_WIDTH), F32),
        scratch_shapes=[pltpu.VMEM((S5_OCT_IN, S5_OCT_IN), BF16)],
        compiler_params=_cparams("arbitrary", "arbitrary"),
        name="s5_out",
    )(z, x_re, x_im, mats["m_pad"], mats["ko"])

    tm = t["tm_glu"]
    return pl.pallas_call(
        _glu_body,
        grid=(seq // tm,),
        in_specs=[pl.BlockSpec((tm, BRANCH_WIDTH), lambda i: (i, 0)),
                  pl.BlockSpec((BRANCH_WIDTH, BRANCH_WIDTH), lambda i: (0, 0))],
        out_specs=pl.BlockSpec((tm, BRANCH_WIDTH), lambda i: (i, 0)),
        out_shape=jax.ShapeDtypeStruct((seq, BRANCH_WIDTH), BF16),
        compiler_params=_cparams("arbitrary"),
        name="s5_glu",
    )(y, w_glu)


def _merge_body(h_ref, ya_ref, yb_ref, yc_ref, yd_ref, wg_ref, wb_ref, o_ref, acc_ref):
    b = pl.program_id(1)
    y = jnp.where(b == 0, ya_ref[...],
                  jnp.where(b == 1, yb_ref[...], jnp.where(b == 2, yc_ref[...], yd_ref[...])))
    term = _sigmoid(_dot(h_ref[...], wg_ref[...])) * _dot(y, wb_ref[...])

    @pl.when(b == 0)
    def _():
        acc_ref[...] = term

    @pl.when(b > 0)
    def _():
        acc_ref[...] += term

    @pl.when(b == N_BRANCHES - 1)
    def _():
        o_ref[...] = acc_ref[...].astype(o_ref.dtype)


def _merge(h, ys, w_gate, w_branch, t):
    seq = h.shape[0]
    tm = t["tm_merge"]
    y_spec = pl.BlockSpec((tm, BRANCH_WIDTH), lambda i, b: (i, 0))
    return pl.pallas_call(
        _merge_body,
        grid=(seq // tm, N_BRANCHES),
        in_specs=[
            pl.BlockSpec((tm, D_MODEL), lambda i, b: (i, 0)),
            y_spec, y_spec, y_spec, y_spec,
            pl.BlockSpec((D_MODEL, D_MODEL), lambda i, b: (0, b)),
            pl.BlockSpec((None, BRANCH_WIDTH, D_MODEL), lambda i, b: (b, 0, 0)),
        ],
        out_specs=pl.BlockSpec((tm, D_MODEL), lambda i, b: (i, 0)),
        out_shape=jax.ShapeDtypeStruct((seq, D_MODEL), BF16),
        scratch_shapes=[pltpu.VMEM((tm, D_MODEL), F32)],
        compiler_params=_cparams("arbitrary", "arbitrary"),
        name="gated_merge",
    )(h, *ys, w_gate, w_branch)


def _out_body(m_ref, w_ref, x_ref, gpost_ref, gpre_ref, x1_ref, h2_ref):
    x1 = x_ref[...] + _rms(_dot(m_ref[...], w_ref[...]), gpost_ref[...])
    x1_ref[...] = x1
    h2_ref[...] = _rms(x1, gpre_ref[...]).astype(h2_ref.dtype)


def _out_proj(merged, w_out, x, g_post, g_pre, t):
    seq = x.shape[0]
    tm = t["tm_out"]
    row = pl.BlockSpec((tm, D_MODEL), lambda i: (i, 0))
    vec = pl.BlockSpec((1, D_MODEL), lambda i: (0, 0))
    return pl.pallas_call(
        _out_body,
        grid=(seq // tm,),
        in_specs=[row, pl.BlockSpec((D_MODEL, D_MODEL), lambda i: (0, 0)), row, vec, vec],
        out_specs=[row, row],
        out_shape=[jax.ShapeDtypeStruct((seq, D_MODEL), F32),
                   jax.ShapeDtypeStruct((seq, D_MODEL), BF16)],
        compiler_params=_cparams("arbitrary"),
        name="out_proj",
    )(merged, w_out, x, g_post, g_pre)


def _ffn_body(h_ref, wg_ref, wv_ref, cg_ref, cv_ref, wd_ref, x_ref, gp_ref, o_ref,
              acc_ref, carg_ref, carv_ref, *, tm):
    i, j = pl.program_id(0), pl.program_id(1)

    @pl.when(j == 0)
    def _():
        acc_ref[...] = jnp.zeros(acc_ref.shape, F32)

    n_parts = 4 if tm % (4 * SUBLANES) == 0 else 1
    pr = tm // n_parts
    def up_proj(a):
        h = h_ref[a * pr:(a + 1) * pr, :]
        return _dot(h, wg_ref[...]), _dot(h, wv_ref[...])

    def conv(up, prev, c_ref):
        row = _iota(up.shape, 0)
        up1 = jnp.where(row == 0, prev[SUBLANES - 1:SUBLANES], pltpu.roll(up, 1, axis=0))
        up2 = jnp.where(row == 0, prev[SUBLANES - 2:SUBLANES - 1],
                        jnp.where(row == 1, prev[SUBLANES - 1:SUBLANES], pltpu.roll(up, 2, axis=0)))
        cw = c_ref[...]
        return cw[2:3] * up + cw[1:2] * up1 + cw[0:1] * up2

    prev_g = jnp.where(i == 0, 0.0, carg_ref[j])
    prev_v = jnp.where(i == 0, 0.0, carv_ref[j])
    up_next = up_proj(0)
    for a in range(n_parts):
        up_g, up_v = up_next
        if a + 1 < n_parts:
            up_next = up_proj(a + 1)
        gate = conv(up_g, prev_g, cg_ref)
        val = conv(up_v, prev_v, cv_ref)
        part = _dot((gate * _sigmoid(gate) * val).astype(BF16), wd_ref[...])
        acc_ref[a * pr:(a + 1) * pr, :] += part
        prev_g, prev_v = up_g[pr - SUBLANES:pr, :], up_v[pr - SUBLANES:pr, :]
    carg_ref[j] = prev_g
    carv_ref[j] = prev_v

    @pl.when(j == pl.num_programs(1) - 1)
    def _():
        o_ref[...] = x_ref[...] + _rms(acc_ref[...], gp_ref[...])


def _ffn(h2, x1, w_gate, w_val, conv_gate, conv_val, w_down, g_post, t):
    seq = x1.shape[0]
    tm, tf = t["tm_ffn"], t["tf_ffn"]
    nf = D_FF_PAD // tf
    row = pl.BlockSpec((tm, D_MODEL), lambda i, j: (i, 0))
    return pl.pallas_call(
        functools.partial(_ffn_body, tm=tm),
        grid=(seq // tm, nf),
        in_specs=[
            row,
            pl.BlockSpec((D_MODEL, tf), lambda i, j: (0, j)),
            pl.BlockSpec((D_MODEL, tf), lambda i, j: (0, j)),
            pl.BlockSpec((FFN_CONV, tf), lambda i, j: (0, j)),
            pl.BlockSpec((FFN_CONV, tf), lambda i, j: (0, j)),
            pl.BlockSpec((tf, D_MODEL), lambda i, j: (j, 0)),
            row,
            pl.BlockSpec((1, D_MODEL), lambda i, j: (0, 0)),
        ],
        out_specs=row,
        out_shape=jax.ShapeDtypeStruct((seq, D_MODEL), F32),
        scratch_shapes=[
            pltpu.VMEM((tm, D_MODEL), F32),
            pltpu.VMEM((nf, SUBLANES, tf), F32),
            pltpu.VMEM((nf, SUBLANES, tf), F32),
        ],
        compiler_params=_cparams("arbitrary", "arbitrary"),
        name="conv_ffn",
    )(h2, w_gate, w_val, conv_gate, conv_val, w_down, x1, g_post)


W_IN_COLS = 13852
O_FF, O_GDN, O_BA, O_SU, O_DR, O_GATE = 1536, 1540, 3588, 3596, 5644, 5660


def _w_in_split_body(w_ref, mix_ref, small_ref, gate_ref):
    bf = lambda a: a.astype(BF16)
    mix_ref[:, 0:COL_GDN_QKV] = bf(w_ref[:, 0:O_FF])
    mix_ref[:, COL_GDN_QKV:COL_S5] = bf(w_ref[:, O_GDN:O_BA])
    mix_ref[:, COL_S5:MIX_WIDTH] = bf(w_ref[:, O_SU:O_DR])
    gate_ref[...] = bf(w_ref[:, O_GATE:W_IN_COLS])
    lane = _iota((w_ref.shape[0], LANES), 1)
    tile = lambda off: w_ref[:, off:off + LANES]
    t_dr, t_ff, t_ab = tile(O_DR - O_DR % LANES), tile(O_FF), tile(O_BA - O_BA % LANES)
    small = jnp.where(lane < SM_FF, pltpu.roll(t_dr, LANES - O_DR % LANES, axis=1), 0.0)
    small = small + jnp.where((lane >= SM_FF) & (lane < SM_BA), pltpu.roll(t_ff, SM_FF, axis=1), 0.0)
    small = small + jnp.where((lane >= SM_BA) & (lane < SM_BA + 2 * N_HEADS),
                              pltpu.roll(t_ab, SM_BA - O_BA % LANES, axis=1), 0.0)
    small_ref[...] = bf(small)


def _w_in_split(w_in):
    depth = w_in.shape[0]
    tr = 128
    steps = D_MODEL // tr
    widths = (MIX_WIDTH, LANES, N_BRANCHES * D_MODEL)
    w2d = w_in.reshape(depth * D_MODEL, W_IN_COLS)
    layers = []
    for layer in range(depth):
        layers.append(pl.pallas_call(
            _w_in_split_body,
            grid=(steps,),
            in_specs=[pl.BlockSpec((tr, W_IN_COLS), lambda i, layer=layer: (layer * steps + i, 0))],
            out_specs=[pl.BlockSpec((tr, w), lambda i: (i, 0)) for w in widths],
            out_shape=[jax.ShapeDtypeStruct((D_MODEL, w), BF16) for w in widths],
            compiler_params=_cparams("arbitrary"),
            name="w_in_split",
        )(w2d))
    return layers


def _layer_weights(gla_w_gk, w_branch, w_out, ffn_w_up, ffn_conv, ffn_w_down, s5_w_glu):
    w_gk_pad = jnp.concatenate(
        [gla_w_gk, jnp.zeros((LANES - GLA_RANK, GLA_K_WIDTH), F32)], axis=0).astype(BF16)
    pad = D_FF_PAD - D_FF
    padc = lambda a: jnp.pad(a, ((0, 0), (0, pad)))
    return dict(
        w_gk_pad=w_gk_pad,
        w_branch=w_branch.astype(BF16), w_out=w_out.astype(BF16),
        w_glu=s5_w_glu.astype(BF16),
        ffn_wg=padc(ffn_w_up[:, :D_FF]).astype(BF16), ffn_wv=padc(ffn_w_up[:, D_FF:]).astype(BF16),
        ffn_cg=padc(ffn_conv[:, :D_FF]), ffn_cv=padc(ffn_conv[:, D_FF:]),
        ffn_wd=jnp.pad(ffn_w_down, ((0, pad), (0, 0))).astype(BF16),
    )


def _layer(x, p, t):
    w = _layer_weights(p["gla_w_gk"], p["w_branch"], p["w_out"], p["ffn_w_up"],
                       p["ffn_conv"], p["ffn_w_down"], p["s5_w_glu"])
    row = lambda a: a.reshape(1, -1)
    z, zs, zst, h = _in_proj(x, row(p["norm_mix_pre"]), p["w_mix"], p["w_small"], t)
    y_a = _fox_attention_t(*_fox_prep(z, zs, p["fox_f_bias"], t), t)
    y_b = _gdn(z, zs, zst, p["gdn_conv"], p["gdn_a_log"], p["gdn_dt_bias"], p["gdn_norm"], t)
    mats = _s5_matrices(p["s5_a_re"], p["s5_a_im"], p["s5_b_re"], p["s5_b_im"], p["s5_c_re"],
                        p["s5_c_im"], p["s5_d"], p["s5_log_dt"])
    y_c = _s5(z, mats, w["w_glu"], t)
    y_d = _gla(z, zs, w["w_gk_pad"], p["gla_b_gk"], p["gla_norm"], t)
    merged = _merge(h, (y_a, y_b, y_c, y_d), p["w_gate"], w["w_branch"], t)
    x1, h2 = _out_proj(merged, w["w_out"], x, row(p["norm_mix_post"]), row(p["norm_ffn_pre"]), t)
    return _ffn(h2, x1, w["ffn_wg"], w["ffn_wv"], w["ffn_cg"], w["ffn_cv"], w["ffn_wd"],
                row(p["norm_ffn_post"]), t)


def kernel(x, norm_mix_pre, norm_mix_post, norm_ffn_pre, norm_ffn_post, w_in, fox_f_bias, gdn_conv, gdn_a_log, gdn_dt_bias, gdn_norm, s5_a_re, s5_a_im, s5_b_re, s5_b_im, s5_c_re, s5_c_im, s5_d, s5_log_dt, s5_w_glu, gla_w_gk, gla_b_gk, gla_norm, w_branch, w_out, ffn_w_up, ffn_conv, ffn_w_down):
    params = dict(
        norm_mix_pre=norm_mix_pre, norm_mix_post=norm_mix_post, norm_ffn_pre=norm_ffn_pre,
        norm_ffn_post=norm_ffn_post, w_in=w_in, fox_f_bias=fox_f_bias, gdn_conv=gdn_conv,
        gdn_a_log=gdn_a_log, gdn_dt_bias=gdn_dt_bias, gdn_norm=gdn_norm, s5_a_re=s5_a_re,
        s5_a_im=s5_a_im, s5_b_re=s5_b_re, s5_b_im=s5_b_im, s5_c_re=s5_c_re, s5_c_im=s5_c_im,
        s5_d=s5_d, s5_log_dt=s5_log_dt, s5_w_glu=s5_w_glu, gla_w_gk=gla_w_gk, gla_b_gk=gla_b_gk,
        gla_norm=gla_norm, w_branch=w_branch, w_out=w_out, ffn_w_up=ffn_w_up, ffn_conv=ffn_conv,
        ffn_w_down=ffn_w_down)
    batch, seq, _ = x.shape
    assert batch == 1
    t = _tiles(seq)
    xs = x[0]
    del params["w_in"]
    w_in_parts = _w_in_split(w_in)
    for layer in range(w_in.shape[0]):
        p = {k: v[layer] for k, v in params.items()}
        p["w_mix"], p["w_small"], p["w_gate"] = w_in_parts[layer]
        xs = _layer(xs, p, t)
    return xs[None]
```

```python
import functools
import math

import jax
import jax.numpy as jnp
from jax import lax
from jax.experimental import pallas as pl
from jax.experimental.pallas import tpu as pltpu

F32 = jnp.float32
BF16 = jnp.bfloat16
HIGHEST = lax.Precision.HIGHEST

D_MODEL = 2048
N_HEADS = 4
HEAD_DIM = 128
BRANCH_WIDTH = 512
GDN_CONV = 4
CHUNK = 64
GDN_SOLVE = 2 * CHUNK
GLA_HEAD_K = 64
GLA_K_WIDTH = 256
GLA_RANK = 16
GLA_GATE_NORMALIZER = 16.0
GLA_SUB = 16
S5_GROUPS = 32
S5_GROUP_DIM = 16
S5_STATE = 64
S5_T = 16
D_FF = 5504
D_FF_PAD = 5632
FFN_CONV = 3
N_BRANCHES = 4
NORM_EPS = 1e-6

LANES = 128
SUBLANES = 8
VMEM_LIMIT_BYTES = 56 * 1024 * 1024

MIX_WIDTH = 5632
COL_FOX = 0
COL_GDN_QKV = 1536
COL_GDN_Z = 3072
COL_S5 = 3584
COL_GLA_Q = 4096
COL_GLA_K = 4352
COL_GLA_V = 4608
COL_GLA_G = 5120
SM_DR = 0
SM_FF = 16
SM_BA = 20
SM_BB = 24
SM_ROWS = 32


def _tiles(seq):
    return dict(
        tm_in=min(1024, seq), tn_in=1408,
        tn_s5=min(256, seq // S5_T),
        t_att=min(2048, seq), strip_att_t=512,
        tm_prep=min(512, seq),
        r_rec=min(256, seq),
        r_gdn=min(256, seq),
        tm_merge=min(512, seq),
        tm_out=min(512, seq),
        tm_ffn=min(512, seq), tf_ffn=512,
        tm_glu=min(1024, seq),
        tn_scan=min(128, seq // S5_T),
    )


def _cparams(*sem):
    return pltpu.CompilerParams(dimension_semantics=sem, vmem_limit_bytes=VMEM_LIMIT_BYTES)


def _sigmoid(x):
    return 1.0 / (1.0 + jnp.exp(-x))


def _softplus(x):
    return jnp.maximum(x, 0.0) + jnp.log1p(jnp.exp(-jnp.abs(x)))


def _log_sigmoid(x):
    return jnp.minimum(x, 0.0) - jnp.log1p(jnp.exp(-jnp.abs(x)))


def _rms(x, gain):
    return x * lax.rsqrt(jnp.mean(x * x, axis=-1, keepdims=True) + NORM_EPS) * gain


def _dot(a, b):
    return jnp.dot(a, b, preferred_element_type=F32)


def _dot_nt(a, b):
    return lax.dot_general(a, b, (((1,), (1,)), ((), ())), preferred_element_type=F32)


def _dot_tn(a, b):
    return lax.dot_general(a, b, (((0,), (0,)), ((), ())), preferred_element_type=F32)


def _iota(shape, axis):
    return lax.broadcasted_iota(jnp.int32, shape, axis)


def _in_proj_body(x_ref, g_ref, w_ref, ws_ref, z_ref, zs_ref, zst_ref, h_ref):
    @pl.when(pl.program_id(1) == 0)
    def _():
        h = _rms(x_ref[...], g_ref[...]).astype(BF16)
        h_ref[...] = h
        zs = _dot(h, ws_ref[...])
        zs_ref[...] = zs
        zst_ref[...] = zs.T[:SM_ROWS, :]

    z_ref[...] = _dot(h_ref[...], w_ref[...])


def _in_proj(x, gain, w_mix, w_small, t):
    seq = x.shape[0]
    tm, tn = t["tm_in"], t["tn_in"]
    return pl.pallas_call(
        _in_proj_body,
        grid=(seq // tm, MIX_WIDTH // tn),
        in_specs=[
            pl.BlockSpec((tm, D_MODEL), lambda i, j: (i, 0)),
            pl.BlockSpec((1, D_MODEL), lambda i, j: (0, 0)),
            pl.BlockSpec((D_MODEL, tn), lambda i, j: (0, j)),
            pl.BlockSpec((D_MODEL, LANES), lambda i, j: (0, 0)),
        ],
        out_specs=[
            pl.BlockSpec((tm, tn), lambda i, j: (i, j)),
            pl.BlockSpec((tm, LANES), lambda i, j: (i, 0)),
            pl.BlockSpec((SM_ROWS, tm), lambda i, j: (0, i)),
            pl.BlockSpec((tm, D_MODEL), lambda i, j: (i, 0)),
        ],
        out_shape=[
            jax.ShapeDtypeStruct((seq, MIX_WIDTH), F32),
            jax.ShapeDtypeStruct((seq, LANES), F32),
            jax.ShapeDtypeStruct((SM_ROWS, seq), F32),
            jax.ShapeDtypeStruct((seq, D_MODEL), BF16),
        ],
        compiler_params=_cparams("arbitrary", "arbitrary"),
        name="in_proj",
    )(x, gain, w_mix, w_small)


FOX_AUG = 2 * HEAD_DIM
LOG2E = 1.4426950408889634


def _fox_prep_body(z_ref, zs_ref, fb_ref, k_ref, qt_ref, vt_ref, carry_ref, *, tm):
    @pl.when(pl.program_id(0) == 0)
    def _():
        carry_ref[...] = jnp.zeros(carry_ref.shape, F32)

    lf = _log_sigmoid(zs_ref[...] + fb_ref[...])
    tri = (_iota((tm, tm), 1) <= _iota((tm, tm), 0)).astype(F32)
    c_col = jnp.dot(tri, lf, precision=HIGHEST, preferred_element_type=F32) + carry_ref[0:1, :]
    carry_ref[0:1, :] = c_col[tm - 1:tm, :]
    lane = _iota((tm, HEAD_DIM), 1)
    ones_t = jnp.where(_iota((HEAD_DIM, tm), 0) < 3, 1.0, 0.0).astype(BF16)
    for h in range(N_HEADS):
        col = lambda base: slice(base + h * HEAD_DIM, base + (h + 1) * HEAD_DIM)
        b = c_col[:, SM_FF + h:SM_FF + h + 1] * (-LOG2E)
        hi = b.astype(BF16).astype(F32)
        mid = (b - hi).astype(BF16).astype(F32)
        lo = b - hi - mid
        aug = jnp.where(lane == 0, hi, jnp.where(lane == 1, mid, jnp.where(lane == 2, lo, 0.0)))
        k_ref[h, :, 0:HEAD_DIM] = z_ref[:, col(BRANCH_WIDTH)].astype(BF16)
        k_ref[h, :, HEAD_DIM:FOX_AUG] = aug.astype(BF16)
        q = (z_ref[:, col(0)] * (HEAD_DIM ** -0.5 * LOG2E)).astype(BF16)
        qt_ref[h, 0:HEAD_DIM, :] = q.astype(F32).T.astype(BF16)
        qt_ref[h, HEAD_DIM:FOX_AUG, :] = ones_t
        vt_ref[h] = z_ref[:, col(2 * BRANCH_WIDTH)].T.astype(BF16)


def _fox_prep(z, zs, f_bias, t):
    seq = z.shape[0]
    tm = t["tm_prep"]
    fb = jnp.zeros((1, LANES), F32).at[0, SM_FF:SM_FF + N_HEADS].set(f_bias)
    return pl.pallas_call(
        functools.partial(_fox_prep_body, tm=tm),
        grid=(seq // tm,),
        in_specs=[pl.BlockSpec((tm, 3 * BRANCH_WIDTH), lambda i: (i, COL_FOX // (3 * BRANCH_WIDTH))),
                  pl.BlockSpec((tm, LANES), lambda i: (i, 0)),
                  pl.BlockSpec((1, LANES), lambda i: (0, 0))],
        out_specs=[pl.BlockSpec((N_HEADS, tm, FOX_AUG), lambda i: (0, i, 0)),
                   pl.BlockSpec((N_HEADS, FOX_AUG, tm), lambda i: (0, 0, i)),
                   pl.BlockSpec((N_HEADS, HEAD_DIM, tm), lambda i: (0, 0, i))],
        out_shape=[jax.ShapeDtypeStruct((N_HEADS, seq, FOX_AUG), BF16),
                   jax.ShapeDtypeStruct((N_HEADS, FOX_AUG, seq), BF16),
                   jax.ShapeDtypeStruct((N_HEADS, HEAD_DIM, seq), BF16)],
        scratch_shapes=[pltpu.VMEM((SUBLANES, LANES), F32)],
        compiler_params=_cparams("arbitrary"),
        name="fox_prep",
    )(z, zs, fb)


def _fox_t_body(qi_ref, kj_ref, k_ref, qt_ref, vt_ref, o_ref, m_ref, l_ref, acc_ref, *, strip):
    p = pl.program_id(1)
    qi, kj = qi_ref[p], kj_ref[p]
    tq = qt_ref.shape[1]

    @pl.when(kj == 0)
    def _():
        m_ref[...] = jnp.full(m_ref.shape, -jnp.inf, F32)
        l_ref[...] = jnp.zeros(l_ref.shape, F32)
        acc_ref[...] = jnp.zeros(acc_ref.shape, F32)

    def step(diagonal):
        n_strips = tq // strip
        n_keys = lambda r: (r + 1) * strip if diagonal else k_ref.shape[0]

        def scores(r):
            s = _dot(k_ref[0:n_keys(r), :], qt_ref[:, r * strip:(r + 1) * strip])
            if diagonal:
                s = jnp.where(_iota(s.shape, 0) <= _iota(s.shape, 1) + r * strip, s, -jnp.inf)
            return s

        s_next = scores(0)
        for r in range(n_strips):
            cols = slice(r * strip, (r + 1) * strip)
            s = s_next
            if r + 1 < n_strips:
                s_next = scores(r + 1)
            m_prev = m_ref[0:1, cols]
            m_new = jnp.maximum(m_prev, jnp.max(s, axis=0, keepdims=True))
            alpha = jnp.exp2(m_prev - m_new)
            e = jnp.exp2(s - m_new)
            l_ref[0:1, cols] = alpha * l_ref[0:1, cols] + jnp.sum(e, axis=0, keepdims=True)
            acc_ref[:, cols] = alpha * acc_ref[:, cols] + _dot(vt_ref[:, 0:n_keys(r)], e.astype(BF16))
            m_ref[0:1, cols] = m_new

    @pl.when(kj < qi)
    def _():
        step(False)

    @pl.when(kj == qi)
    def _():
        step(True)
        o_ref[...] = (acc_ref[...] / l_ref[0:1, :]).T.astype(o_ref.dtype)


def _fox_attention_t(k_aug, q_t, v_t, t):
    seq = k_aug.shape[1]
    tq = t["t_att"]
    nq = seq // tq
    pairs = [(i, j) for i in range(nq) for j in range(i + 1)]
    qi = jnp.asarray([a for a, _ in pairs], jnp.int32)
    kj = jnp.asarray([b for _, b in pairs], jnp.int32)
    grid_spec = pltpu.PrefetchScalarGridSpec(
        num_scalar_prefetch=2,
        grid=(N_HEADS, len(pairs)),
        in_specs=[
            pl.BlockSpec((None, tq, FOX_AUG), lambda h, p, qi, kj: (h, kj[p], 0)),
            pl.BlockSpec((None, FOX_AUG, tq), lambda h, p, qi, kj: (h, 0, qi[p])),
            pl.BlockSpec((None, HEAD_DIM, tq), lambda h, p, qi, kj: (h, 0, kj[p])),
        ],
        out_specs=pl.BlockSpec((tq, HEAD_DIM), lambda h, p, qi, kj: (qi[p], h)),
        scratch_shapes=[
            pltpu.VMEM((SUBLANES, tq), F32),
            pltpu.VMEM((SUBLANES, tq), F32),
            pltpu.VMEM((HEAD_DIM, tq), F32),
        ],
    )
    return pl.pallas_call(
        functools.partial(_fox_t_body, strip=min(t["strip_att_t"], tq)),
        grid_spec=grid_spec,
        out_shape=jax.ShapeDtypeStruct((seq, BRANCH_WIDTH), BF16),
        compiler_params=_cparams("arbitrary", "arbitrary"),
        name="fox_attention_t",
    )(qi, kj, k_aug, q_t, v_t)


def _unit_lower_inverses(mats):
    n = mats[0].shape[0]
    eye = (_iota((n, n), 0) == _iota((n, n), 1)).astype(F32)
    ps = [eye - a for a in mats]
    ms = [a.astype(BF16) for a in mats]
    for _ in range(int(math.log2(CHUNK)) - 1):
        ms = [_dot(m, m).astype(BF16) for m in ms]
        ps = [p + _dot(p.astype(BF16), m) for p, m in zip(ps, ms)]
    return ps


def _gdn_body(qkv_ref, bz_ref, zs_ref, zst_ref, cw_ref, alr_ref, alc_ref, dtr_ref, dtc_ref,
              gn_ref, o_ref, ext_ref, s_ref, *, rows):
    @pl.when(pl.program_id(0) == 0)
    def _():
        ext_ref[0:SUBLANES, :] = jnp.zeros((SUBLANES, ext_ref.shape[1]), F32)
        s_ref[...] = jnp.zeros(s_ref.shape, F32)

    x = qkv_ref[...]
    ext_ref[SUBLANES:SUBLANES + rows, :] = x
    cw = cw_ref[...]
    conv = cw[GDN_CONV - 1:GDN_CONV] * x
    for j in range(GDN_CONV - 1):
        off = SUBLANES - (GDN_CONV - 1) + j
        conv = conv + cw[j:j + 1] * ext_ref[off:off + rows, :]
    ext_ref[0:SUBLANES, :] = x[rows - SUBLANES:rows, :]
    qkv = conv * _sigmoid(conv)

    zs = zs_ref[...]
    zst = zst_ref[...]
    g_col = -jnp.exp(alr_ref[...]) * _softplus(zs[:, SM_BA:SM_BA + N_HEADS] + dtr_ref[...])
    g_row = -jnp.exp(alc_ref[...]) * _softplus(zst[SM_BA:SM_BA + N_HEADS, :] + dtc_ref[...])
    beta = _sigmoid(zs[:, SM_BB:SM_BB + N_HEADS])
    rr, cc = _iota((rows, rows), 0), _iota((rows, rows), 1)
    shift = int(math.log2(CHUNK))
    same = lax.shift_right_logical(rr, shift) == lax.shift_right_logical(cc, shift)
    incl = (same & (cc <= rr)).astype(F32)
    gc_col = jnp.dot(incl, g_col, precision=HIGHEST, preferred_element_type=F32)
    gc_row = _dot_nt_highest(g_row, incl)
    strict = (same & (cc < rr))[:GDN_SOLVE, :GDN_SOLVE]
    n_chunks = rows // CHUNK
    rc, ccn = _iota((CHUNK, CHUNK), 0), _iota((CHUNK, CHUNK), 1)
    causal = ccn <= rc

    heads = range(N_HEADS)
    blocks = [slice(b * GDN_SOLVE, (b + 1) * GDN_SOLVE) for b in range(rows // GDN_SOLVE)]
    hs = lambda base, h: slice(base + h * HEAD_DIM, base + (h + 1) * HEAD_DIM)
    qs, ks, kbs, gccs, gcrs, rhss, a_kks = [], [], [], [], [], [], []
    for h in heads:
        q, k, v = qkv[:, hs(0, h)], qkv[:, hs(BRANCH_WIDTH, h)], qkv[:, hs(2 * BRANCH_WIDTH, h)]
        q = q * lax.rsqrt(jnp.sum(q * q, axis=-1, keepdims=True) + NORM_EPS) * (HEAD_DIM ** -0.5)
        k = k * lax.rsqrt(jnp.sum(k * k, axis=-1, keepdims=True) + NORM_EPS)
        gcc = gc_col[:, h:h + 1]
        gcr = gc_row[h:h + 1, :]
        bt = beta[:, h:h + 1]
        kb = k.astype(BF16)
        gam = jnp.exp(gcc)
        qs.append((q, q * gam))
        ks.append(k)
        kbs.append(kb)
        gccs.append(gcc)
        gcrs.append(gcr)
        rhss.append(jnp.concatenate([(bt * gam) * k, bt * v], axis=-1))
        for sl in blocks:
            decay = jnp.exp(jnp.minimum(gcc[sl] - gcr[:, sl], 0.0))
            a_kks.append(jnp.where(strict, bt[sl] * _dot_nt(kb[sl], kb[sl]) * decay, 0.0))
    tinvs = _unit_lower_inverses(a_kks)
    sols = []
    for h in heads:
        parts = [_dot(tinvs[h * len(blocks) + b].astype(BF16), rhss[h][sl].astype(BF16))
                 for b, sl in enumerate(blocks)]
        sols.append(jnp.concatenate(parts, axis=0) if len(parts) > 1 else parts[0])
    states = [s_ref[h] for h in heads]
    outs = [[] for _ in heads]
    for c in range(n_chunks):
        lo, hi = c * CHUNK, (c + 1) * CHUNK
        for h in heads:
            (q, q_dec), k, kb, gcc, gcr = qs[h], ks[h], kbs[h], gccs[h], gcrs[h]
            w_mat, u_mat = sols[h][lo:hi, :HEAD_DIM], sols[h][lo:hi, HEAD_DIM:]
            g_last = gcc[hi - 1:hi, :]
            decay = jnp.exp(jnp.minimum(gcc[lo:hi] - gcr[:, lo:hi], 0.0))
            a_qk = jnp.where(causal, _dot_nt(q[lo:hi].astype(BF16), kb[lo:hi]) * decay, 0.0)
            k_dec = k[lo:hi] * jnp.exp(g_last - gcc[lo:hi])
            sb = states[h].astype(BF16)
            delta = u_mat - _dot(w_mat.astype(BF16), sb)
            db = delta.astype(BF16)
            outs[h].append(_dot(q_dec[lo:hi].astype(BF16), sb) + _dot(a_qk.astype(BF16), db))
            states[h] = jnp.exp(g_last) * states[h] + _dot_tn(k_dec.astype(BF16), db)
    for h in heads:
        s_ref[h] = states[h]
        o = jnp.concatenate(outs[h], axis=0) if n_chunks > 1 else outs[h][0]
        zg = bz_ref[:, h * HEAD_DIM:(h + 1) * HEAD_DIM]
        o_ref[:, h * HEAD_DIM:(h + 1) * HEAD_DIM] = (
            _rms(o, gn_ref[...]) * (zg * _sigmoid(zg))).astype(o_ref.dtype)


def _dot_nt_highest(a, b):
    return lax.dot_general(a, b, (((1,), (1,)), ((), ())), precision=HIGHEST,
                           preferred_element_type=F32)


def _gdn(z, zs, zst, conv_w, a_log, dt_bias, norm_gain, t):
    seq = z.shape[0]
    rows = t["r_gdn"]
    assert rows % GDN_SOLVE == 0
    qkv_w = 3 * BRANCH_WIDTH
    full = lambda shape: pl.BlockSpec(shape, lambda i: (0,) * len(shape))
    return pl.pallas_call(
        functools.partial(_gdn_body, rows=rows),
        grid=(seq // rows,),
        in_specs=[
            pl.BlockSpec((rows, qkv_w), lambda i: (i, COL_GDN_QKV // qkv_w)),
            pl.BlockSpec((rows, BRANCH_WIDTH), lambda i: (i, COL_GDN_Z // BRANCH_WIDTH)),
            pl.BlockSpec((rows, LANES), lambda i: (i, 0)),
            pl.BlockSpec((SM_ROWS, rows), lambda i: (0, i)),
            full((GDN_CONV, qkv_w)),
            full((1, N_HEADS)), full((N_HEADS, 1)), full((1, N_HEADS)), full((N_HEADS, 1)),
            full((1, HEAD_DIM)),
        ],
        out_specs=pl.BlockSpec((rows, BRANCH_WIDTH), lambda i: (i, 0)),
        out_shape=jax.ShapeDtypeStruct((seq, BRANCH_WIDTH), BF16),
        scratch_shapes=[
            pltpu.VMEM((rows + SUBLANES, qkv_w), F32),
            pltpu.VMEM((N_HEADS, HEAD_DIM, HEAD_DIM), F32),
        ],
        compiler_params=_cparams("arbitrary"),
        name="gated_deltanet",
    )(z, z, zs, zst, conv_w, a_log.reshape(1, -1), a_log.reshape(-1, 1),
      dt_bias.reshape(1, -1), dt_bias.reshape(-1, 1), norm_gain.reshape(1, -1))


def _gla_body(q_ref, k_ref, v_ref, g_ref, zs_ref, wgk_ref, bgk_ref, gn_ref, o_ref, s_ref, *, rows):
    @pl.when(pl.program_id(0) == 0)
    def _():
        s_ref[...] = jnp.zeros(s_ref.shape, F32)

    lg = _log_sigmoid(_dot(zs_ref[...].astype(BF16), wgk_ref[...]) + bgk_ref[...])
    lg = lg * (1.0 / GLA_GATE_NORMALIZER)
    rr, cc = _iota((rows, rows), 0), _iota((rows, rows), 1)
    shift = int(math.log2(CHUNK))
    same = lax.shift_right_logical(rr, shift) == lax.shift_right_logical(cc, shift)
    incl = (same & (cc <= rr)).astype(F32)
    gcs = jnp.dot(incl, lg, precision=HIGHEST, preferred_element_type=F32)
    rc, ccn = _iota((CHUNK, CHUNK), 0), _iota((CHUNK, CHUNK), 1)
    causal = ccn <= rc
    n_chunks = rows // CHUNK
    qa = q_ref[...] * (GLA_HEAD_K ** -0.5)
    ka = k_ref[...]

    heads = range(N_HEADS)
    k_slice = lambda h: slice(h * GLA_HEAD_K, (h + 1) * GLA_HEAD_K)
    v_slice = lambda h: slice(h * HEAD_DIM, (h + 1) * HEAD_DIM)
    pairs = [(c, h) for c in range(n_chunks) for h in heads]
    a_mats, kvs, q_decs, vbs, g_lasts = {}, {}, {}, {}, {}
    for c, h in pairs:
        lo, hi = c * CHUNK, (c + 1) * CHUNK
        q, k, g = qa[lo:hi, k_slice(h)], ka[lo:hi, k_slice(h)], gcs[lo:hi, k_slice(h)]
        v = v_ref[lo:hi, v_slice(h)].astype(BF16)
        g_last = g[CHUNK - 1:CHUNK, :]
        blocks = []
        for sblk in range(CHUNK // GLA_SUB):
            a, b = sblk * GLA_SUB, (sblk + 1) * GLA_SUB
            ref = g[a:a + 1, :]
            qt = q[a:b] * jnp.exp(g[a:b] - ref)
            kt = k * jnp.exp(jnp.minimum(ref - g, 80.0))
            blocks.append(_dot_nt(qt.astype(BF16), kt.astype(BF16)))
        a_mats[c, h] = jnp.where(causal, jnp.concatenate(blocks, axis=0), 0.0).astype(BF16)
        kvs[c, h] = _dot_tn(v, (k * jnp.exp(g_last - g)).astype(BF16))
        q_decs[c, h] = (q * jnp.exp(g)).astype(BF16)
        vbs[c, h], g_lasts[c, h] = v, g_last
    states = [s_ref[h] for h in heads]
    outs = [[] for _ in heads]
    for c, h in pairs:
        inter = _dot_nt(q_decs[c, h], states[h].astype(BF16))
        outs[h].append(inter + _dot(a_mats[c, h], vbs[c, h]))
        states[h] = jnp.exp(g_lasts[c, h]) * states[h] + kvs[c, h]
    for h in heads:
        vs = v_slice(h)
        s_ref[h] = states[h]
        o = jnp.concatenate(outs[h], axis=0) if n_chunks > 1 else outs[h][0]
        gate = g_ref[:, vs]
        o_ref[:, vs] = (_rms(o, gn_ref[...]) * (gate * _sigmoid(gate))).astype(o_ref.dtype)


def _gla(z, zs, w_gk_pad, b_gk, norm_gain, t):
    seq = z.shape[0]
    rows = t["r_rec"]
    full = lambda shape: pl.BlockSpec(shape, lambda i: (0,) * len(shape))
    return pl.pallas_call(
        functools.partial(_gla_body, rows=rows),
        grid=(seq // rows,),
        in_specs=[
            pl.BlockSpec((rows, GLA_K_WIDTH), lambda i: (i, COL_GLA_Q // GLA_K_WIDTH)),
            pl.BlockSpec((rows, GLA_K_WIDTH), lambda i: (i, COL_GLA_K // GLA_K_WIDTH)),
            pl.BlockSpec((rows, BRANCH_WIDTH), lambda i: (i, COL_GLA_V // BRANCH_WIDTH)),
            pl.BlockSpec((rows, BRANCH_WIDTH), lambda i: (i, COL_GLA_G // BRANCH_WIDTH)),
            pl.BlockSpec((rows, LANES), lambda i: (i, 0)),
            full((LANES, GLA_K_WIDTH)), full((1, GLA_K_WIDTH)), full((1, HEAD_DIM)),
        ],
        out_specs=pl.BlockSpec((rows, BRANCH_WIDTH), lambda i: (i, 0)),
        out_shape=jax.ShapeDtypeStruct((seq, BRANCH_WIDTH), BF16),
        scratch_shapes=[pltpu.VMEM((N_HEADS, HEAD_DIM, GLA_HEAD_K), F32)],
        compiler_params=_cparams("arbitrary"),
        name="gla",
    )(z, z, z, z, zs, w_gk_pad, b_gk.reshape(1, -1), norm_gain.reshape(1, -1))


S5_OCT = LANES // S5_GROUP_DIM
S5_N_OCT = S5_GROUPS // S5_OCT
S5_OCT_IN = S5_T * LANES
S5_OCT_STATE = S5_OCT * S5_STATE
S5_STATE_WIDTH = S5_GROUPS * S5_STATE


def _s5_matrices(a_re, a_im, b_re, b_im, c_re, c_im, d, log_dt):
    g, p, hg, tt, no, oc = S5_GROUPS, S5_STATE, S5_GROUP_DIM, S5_T, S5_N_OCT, S5_OCT
    lam = lax.complex(a_re.astype(F32), a_im.astype(F32))
    ldt = lam * jnp.exp(log_dt.astype(F32))[:, None]
    lam_bar = jnp.exp(ldt)
    b_bar = ((lam_bar - 1.0) / lam)[..., None] * lax.complex(b_re.astype(F32), b_im.astype(F32))
    c = lax.complex(c_re.astype(F32), c_im.astype(F32))
    steps = jnp.arange(tt + 1, dtype=F32)
    pw = jnp.exp(ldt[None] * steps[:, None, None])
    k_lag = jnp.einsum("ghp,lgp,gpk->gklh", c, pw[:tt], b_bar, precision=HIGHEST).real
    k_lag = k_lag.at[:, :, 0, :].add(jnp.eye(hg, dtype=F32)[None] * d.astype(F32).reshape(g, 1, hg))
    kc = k_lag.reshape(no, LANES, tt * hg)
    tau = jnp.arange(tt)
    w = pw[tt - 1 - tau][:, :, None, :] * b_bar.transpose(0, 2, 1)[None]
    w = jnp.stack([w.real, w.imag], axis=3)
    wc = w.reshape(tt, no, oc, hg, 2 * p).transpose(1, 0, 2, 3, 4).reshape(no, S5_OCT_IN, 2 * p)
    ko = c.transpose(0, 2, 1)[:, :, None, :] * pw[1:tt + 1].transpose(1, 2, 0)[:, :, :, None]
    ko = jnp.stack([ko.real, -ko.imag], axis=0)
    koc = ko.reshape(2, no, oc * p, tt * hg).transpose(1, 0, 2, 3).reshape(no, 2 * S5_OCT_STATE, tt * hg)
    lam_t = pw[tt].reshape(1, g * p)
    m_pad, w_cat, ko_cat = _s5_expand(kc, wc, koc)
    return dict(m_pad=m_pad, w=w_cat, ko=ko_cat, lam_re=lam_t.real, lam_im=lam_t.imag)


def _s5_expand_body(kc_ref, wc_ref, koc_ref, mpad_ref, w_ref, ko_ref):
    hg, p = S5_GROUP_DIM, S5_STATE
    sh = lambda x, n: lax.shift_right_logical(x, int(math.log2(n)))
    md = lambda x, n: jnp.bitwise_and(x, n - 1)

    def replicate(src, out_cols, unit):
        n_in = src.shape[1]
        a, c = _iota((n_in, out_cols), 0), _iota((n_in, out_cols), 1)
        rep = (sh(a, unit) == sh(c, unit * S5_OCT)) & (md(a, unit) == md(c, unit))
        return _dot(src.astype(BF16), rep.astype(BF16))

    def keep_diagonal(x, row_unit, col_unit):
        r, c = _iota(x.shape, 0), _iota(x.shape, 1)
        same = md(sh(r, row_unit), S5_OCT) == md(sh(c, col_unit), S5_OCT)
        return jnp.where(same, x, 0.0).astype(BF16)

    m = keep_diagonal(replicate(kc_ref[...], S5_OCT_IN, hg), hg, hg)
    mpad_ref[:, 0:(S5_T - 1) * LANES] = jnp.zeros((LANES, (S5_T - 1) * LANES), BF16)
    mpad_ref[:, (S5_T - 1) * LANES:] = m
    w_ref[...] = keep_diagonal(replicate(wc_ref[...], 2 * S5_OCT_STATE, p), hg, p)
    ko_ref[...] = keep_diagonal(replicate(koc_ref[...], S5_OCT_IN, hg), p, hg)


def _s5_expand(kc, wc, koc):
    no = S5_N_OCT
    shapes = ((LANES, (2 * S5_T - 1) * LANES), (S5_OCT_IN, 2 * S5_OCT_STATE),
              (2 * S5_OCT_STATE, S5_OCT_IN))
    spec = lambda a, b: pl.BlockSpec((None, a, b), lambda o: (o, 0, 0))
    return pl.pallas_call(
        _s5_expand_body,
        grid=(no,),
        in_specs=[spec(*kc.shape[1:]), spec(*wc.shape[1:]), spec(*koc.shape[1:])],
        out_specs=[spec(*s) for s in shapes],
        out_shape=[jax.ShapeDtypeStruct((no,) + s, BF16) for s in shapes],
        compiler_params=_cparams("arbitrary"),
        name="s5_expand",
    )(kc, wc, koc)


def _s5_gather(u_ref, tn):
    return jnp.concatenate(
        [u_ref[pl.ds(s, tn, stride=S5_T), :] for s in range(S5_T)], axis=1).astype(BF16)


def _s5_local_body(u_ref, w_ref, vre_ref, vim_ref, *, tn):
    v = _dot(_s5_gather(u_ref, tn), w_ref[...])
    vre_ref[...] = v[:, :S5_OCT_STATE]
    vim_ref[...] = v[:, S5_OCT_STATE:]


def _s5_scan_body(vre_ref, vim_ref, lre_ref, lim_ref, xre_ref, xim_ref, sre_ref, sim_ref, *, tn):
    @pl.when(pl.program_id(0) == 0)
    def _():
        sre_ref[...] = jnp.zeros(sre_ref.shape, F32)
        sim_ref[...] = jnp.zeros(sim_ref.shape, F32)

    lre, lim = lre_ref[...], lim_ref[...]

    def body(m, carry):
        xr, xi = carry
        base = pl.multiple_of(m * SUBLANES, SUBLANES)
        vr = vre_ref[pl.ds(base, SUBLANES), :]
        vi = vim_ref[pl.ds(base, SUBLANES), :]
        out_r, out_i = [], []
        for j in range(SUBLANES):
            out_r.append(xr)
            out_i.append(xi)
            xr, xi = (lre * xr - lim * xi + vr[j:j + 1], lre * xi + lim * xr + vi[j:j + 1])
        xre_ref[pl.ds(base, SUBLANES), :] = jnp.concatenate(out_r, axis=0)
        xim_ref[pl.ds(base, SUBLANES), :] = jnp.concatenate(out_i, axis=0)
        return xr, xi

    xr, xi = lax.fori_loop(0, tn // SUBLANES, body, (sre_ref[...], sim_ref[...]))
    sre_ref[...] = xr
    sim_ref[...] = xi


def _gelu_tanh(x):
    return 0.5 * x * (1.0 + jnp.tanh(math.sqrt(2.0 / math.pi) * (x + 0.044715 * (x * x * x))))


def _s5_out_body(u_ref, xre_ref, xim_ref, mpad_ref, ko_ref, y_ref, ki_ref, *, tn):
    @pl.when(pl.program_id(1) == 0)
    def _():
        for s in range(S5_T):
            off = (S5_T - 1 - s) * LANES
            ki_ref[s * LANES:(s + 1) * LANES, :] = mpad_ref[:, off:off + S5_OCT_IN]

    x = jnp.concatenate([xre_ref[...], xim_ref[...]], axis=1).astype(BF16)
    y = _gelu_tanh(_dot(_s5_gather(u_ref, tn), ki_ref[...]) + _dot(x, ko_ref[...]))
    for s in range(S5_T):
        y_ref[pl.ds(s, tn, stride=S5_T), :] = y[:, s * LANES:(s + 1) * LANES]


def _glu_body(y_ref, w_ref, o_ref):
    y = y_ref[...]
    o_ref[...] = (y * _sigmoid(_dot(y.astype(BF16), w_ref[...]))).astype(o_ref.dtype)


def _s5(z, mats, w_glu, t):
    seq = z.shape[0]
    nb = seq // S5_T
    tb = t["tn_s5"]
    col0 = COL_S5 // LANES
    u_spec = pl.BlockSpec((S5_T * tb, LANES), lambda o, n: (n, col0 + o))
    st_spec = pl.BlockSpec((tb, S5_OCT_STATE), lambda o, n: (n, o))
    oct_mat = lambda a, b: pl.BlockSpec((None, a, b), lambda o, n: (o, 0, 0))
    v_re, v_im = pl.pallas_call(
        functools.partial(_s5_local_body, tn=tb),
        grid=(S5_N_OCT, nb // tb),
        in_specs=[u_spec, oct_mat(S5_OCT_IN, 2 * S5_OCT_STATE)],
        out_specs=[st_spec, st_spec],
        out_shape=[jax.ShapeDtypeStruct((nb, S5_STATE_WIDTH), F32)] * 2,
        compiler_params=_cparams("arbitrary", "arbitrary"),
        name="s5_local",
    )(z, mats["w"])

    tn = t["tn_scan"]
    rowblk = pl.BlockSpec((tn, S5_STATE_WIDTH), lambda i: (i, 0))
    vec = pl.BlockSpec((1, S5_STATE_WIDTH), lambda i: (0, 0))
    x_re, x_im = pl.pallas_call(
        functools.partial(_s5_scan_body, tn=tn),
        grid=(nb // tn,),
        in_specs=[rowblk, rowblk, vec, vec],
        out_specs=[rowblk, rowblk],
        out_shape=[jax.ShapeDtypeStruct((nb, S5_STATE_WIDTH), F32)] * 2,
        scratch_shapes=[pltpu.VMEM((1, S5_STATE_WIDTH), F32)] * 2,
        compiler_params=_cparams("arbitrary"),
        name="s5_scan",
    )(v_re, v_im, mats["lam_re"], mats["lam_im"])

    y = pl.pallas_call(
        functools.partial(_s5_out_body, tn=tb),
        grid=(S5_N_OCT, nb // tb),
        in_specs=[u_spec, st_spec, st_spec,
                  oct_mat(LANES, (2 * S5_T - 1) * LANES), oct_mat(2 * S5_OCT_STATE, S5_OCT_IN)],
        out_specs=pl.BlockSpec((S5_T * tb, LANES), lambda o, n: (n, o)),
        out_shape=jax.ShapeDtypeStruct((seq, BRANCH_WIDTH), F32),
        scratch_shapes=[pltpu.VMEM((S5_OCT_IN, S5_OCT_IN), BF16)],
        compiler_params=_cparams("arbitrary", "arbitrary"),
        name="s5_out",
    )(z, x_re, x_im, mats["m_pad"], mats["ko"])

    tm = t["tm_glu"]
    return pl.pallas_call(
        _glu_body,
        grid=(seq // tm,),
        in_specs=[pl.BlockSpec((tm, BRANCH_WIDTH), lambda i: (i, 0)),
                  pl.BlockSpec((BRANCH_WIDTH, BRANCH_WIDTH), lambda i: (0, 0))],
        out_specs=pl.BlockSpec((tm, BRANCH_WIDTH), lambda i: (i, 0)),
        out_shape=jax.ShapeDtypeStruct((seq, BRANCH_WIDTH), BF16),
        compiler_params=_cparams("arbitrary"),
        name="s5_glu",
    )(y, w_glu)


def _merge_body(h_ref, ya_ref, yb_ref, yc_ref, yd_ref, wg_ref, wb_ref, o_ref, acc_ref):
    b = pl.program_id(1)
    y = jnp.where(b == 0, ya_ref[...],
                  jnp.where(b == 1, yb_ref[...], jnp.where(b == 2, yc_ref[...], yd_ref[...])))
    term = _sigmoid(_dot(h_ref[...], wg_ref[...])) * _dot(y, wb_ref[...])

    @pl.when(b == 0)
    def _():
        acc_ref[...] = term

    @pl.when(b > 0)
    def _():
        acc_ref[...] += term

    @pl.when(b == N_BRANCHES - 1)
    def _():
        o_ref[...] = acc_ref[...].astype(o_ref.dtype)


def _merge(h, ys, w_gate, w_branch, t):
    seq = h.shape[0]
    tm = t["tm_merge"]
    y_spec = pl.BlockSpec((tm, BRANCH_WIDTH), lambda i, b: (i, 0))
    return pl.pallas_call(
        _merge_body,
        grid=(seq // tm, N_BRANCHES),
        in_specs=[
            pl.BlockSpec((tm, D_MODEL), lambda i, b: (i, 0)),
            y_spec, y_spec, y_spec, y_spec,
            pl.BlockSpec((D_MODEL, D_MODEL), lambda i, b: (0, b)),
            pl.BlockSpec((None, BRANCH_WIDTH, D_MODEL), lambda i, b: (b, 0, 0)),
        ],
        out_specs=pl.BlockSpec((tm, D_MODEL), lambda i, b: (i, 0)),
        out_shape=jax.ShapeDtypeStruct((seq, D_MODEL), BF16),
        scratch_shapes=[pltpu.VMEM((tm, D_MODEL), F32)],
        compiler_params=_cparams("arbitrary", "arbitrary"),
        name="gated_merge",
    )(h, *ys, w_gate, w_branch)


def _out_body(m_ref, w_ref, x_ref, gpost_ref, gpre_ref, x1_ref, h2_ref):
    x1 = x_ref[...] + _rms(_dot(m_ref[...], w_ref[...]), gpost_ref[...])
    x1_ref[...] = x1
    h2_ref[...] = _rms(x1, gpre_ref[...]).astype(h2_ref.dtype)


def _out_proj(merged, w_out, x, g_post, g_pre, t):
    seq = x.shape[0]
    tm = t["tm_out"]
    row = pl.BlockSpec((tm, D_MODEL), lambda i: (i, 0))
    vec = pl.BlockSpec((1, D_MODEL), lambda i: (0, 0))
    return pl.pallas_call(
        _out_body,
        grid=(seq // tm,),
        in_specs=[row, pl.BlockSpec((D_MODEL, D_MODEL), lambda i: (0, 0)), row, vec, vec],
        out_specs=[row, row],
        out_shape=[jax.ShapeDtypeStruct((seq, D_MODEL), F32),
                   jax.ShapeDtypeStruct((seq, D_MODEL), BF16)],
        compiler_params=_cparams("arbitrary"),
        name="out_proj",
    )(merged, w_out, x, g_post, g_pre)


def _ffn_body(h_ref, wg_ref, wv_ref, cg_ref, cv_ref, wd_ref, x_ref, gp_ref, o_ref,
              acc_ref, carg_ref, carv_ref, *, tm):
    i, j = pl.program_id(0), pl.program_id(1)

    @pl.when(j == 0)
    def _():
        acc_ref[...] = jnp.zeros(acc_ref.shape, F32)

    n_parts = 4 if tm % (4 * SUBLANES) == 0 else 1
    pr = tm // n_parts
    def up_proj(a):
        h = h_ref[a * pr:(a + 1) * pr, :]
        return _dot(h, wg_ref[...]), _dot(h, wv_ref[...])

    def conv(up, prev, c_ref):
        row = _iota(up.shape, 0)
        up1 = jnp.where(row == 0, prev[SUBLANES - 1:SUBLANES], pltpu.roll(up, 1, axis=0))
        up2 = jnp.where(row == 0, prev[SUBLANES - 2:SUBLANES - 1],
                        jnp.where(row == 1, prev[SUBLANES - 1:SUBLANES], pltpu.roll(up, 2, axis=0)))
        cw = c_ref[...]
        return cw[2:3] * up + cw[1:2] * up1 + cw[0:1] * up2

    prev_g = jnp.where(i == 0, 0.0, carg_ref[j])
    prev_v = jnp.where(i == 0, 0.0, carv_ref[j])
    up_next = up_proj(0)
    for a in range(n_parts):
        up_g, up_v = up_next
        if a + 1 < n_parts:
            up_next = up_proj(a + 1)
        gate = conv(up_g, prev_g, cg_ref)
        val = conv(up_v, prev_v, cv_ref)
        part = _dot((gate * _sigmoid(gate) * val).astype(BF16), wd_ref[...])
        acc_ref[a * pr:(a + 1) * pr, :] += part
        prev_g, prev_v = up_g[pr - SUBLANES:pr, :], up_v[pr - SUBLANES:pr, :]
    carg_ref[j] = prev_g
    carv_ref[j] = prev_v

    @pl.when(j == pl.num_programs(1) - 1)
    def _():
        o_ref[...] = x_ref[...] + _rms(acc_ref[...], gp_ref[...])


def _ffn(h2, x1, w_gate, w_val, conv_gate, conv_val, w_down, g_post, t):
    seq = x1.shape[0]
    tm, tf = t["tm_ffn"], t["tf_ffn"]
    nf = D_FF_PAD // tf
    row = pl.BlockSpec((tm, D_MODEL), lambda i, j: (i, 0))
    return pl.pallas_call(
        functools.partial(_ffn_body, tm=tm),
        grid=(seq // tm, nf),
        in_specs=[
            row,
            pl.BlockSpec((None, D_MODEL, tf), lambda i, j: (j, 0, 0)),
            pl.BlockSpec((None, D_MODEL, tf), lambda i, j: (j, 0, 0)),
            pl.BlockSpec((FFN_CONV, tf), lambda i, j: (0, j)),
            pl.BlockSpec((FFN_CONV, tf), lambda i, j: (0, j)),
            pl.BlockSpec((tf, D_MODEL), lambda i, j: (j, 0)),
            row,
            pl.BlockSpec((1, D_MODEL), lambda i, j: (0, 0)),
        ],
        out_specs=row,
        out_shape=jax.ShapeDtypeStruct((seq, D_MODEL), F32),
        scratch_shapes=[
            pltpu.VMEM((tm, D_MODEL), F32),
            pltpu.VMEM((nf, SUBLANES, tf), F32),
            pltpu.VMEM((nf, SUBLANES, tf), F32),
        ],
        compiler_params=_cparams("arbitrary", "arbitrary"),
        name="conv_ffn",
    )(h2, w_gate, w_val, conv_gate, conv_val, w_down, x1, g_post)


W_IN_COLS = 13852
O_FF, O_GDN, O_BA, O_SU, O_DR, O_GATE = 1536, 1540, 3588, 3596, 5644, 5660


def _w_in_split_body(w_ref, mix_ref, small_ref, gate_ref):
    tr = w_ref.shape[0]
    bf = lambda a: a.astype(BF16)
    mix_ref[:, 0:COL_GDN_QKV] = bf(w_ref[:, 0:O_FF])
    mix_ref[:, COL_GDN_QKV:COL_S5] = bf(w_ref[:, O_GDN:O_BA])
    mix_ref[:, COL_S5:MIX_WIDTH] = bf(w_ref[:, O_SU:O_DR])
    gate_ref[...] = bf(w_ref[:, O_GATE:W_IN_COLS])
    lane = _iota((tr, LANES), 1)
    tile = lambda off: w_ref[:, off:off + LANES]
    t_dr, t_ff, t_ab = tile(O_DR - O_DR % LANES), tile(O_FF), tile(O_BA - O_BA % LANES)
    small = jnp.where(lane < SM_FF, pltpu.roll(t_dr, LANES - O_DR % LANES, axis=1), 0.0)
    small = small + jnp.where((lane >= SM_FF) & (lane < SM_BA), pltpu.roll(t_ff, SM_FF, axis=1), 0.0)
    small = small + jnp.where((lane >= SM_BA) & (lane < SM_BA + 2 * N_HEADS),
                              pltpu.roll(t_ab, SM_BA - O_BA % LANES, axis=1), 0.0)
    small_ref[...] = bf(small)


def _w_in_split(w_in):
    depth = w_in.shape[0]
    tr = 128
    steps = D_MODEL // tr
    widths = (MIX_WIDTH, LANES, N_BRANCHES * D_MODEL)
    w2d = w_in.reshape(depth * D_MODEL, W_IN_COLS)
    layers = []
    for layer in range(depth):
        layers.append(pl.pallas_call(
            _w_in_split_body,
            grid=(steps,),
            in_specs=[pl.BlockSpec((tr, W_IN_COLS), lambda i, layer=layer: (layer * steps + i, 0))],
            out_specs=[pl.BlockSpec((tr, w), lambda i: (i, 0)) for w in widths],
            out_shape=[jax.ShapeDtypeStruct((D_MODEL, w), BF16) for w in widths],
            compiler_params=_cparams("arbitrary"),
            name="w_in_split",
        )(w2d))
    return layers


def _ffn_up_split_body(w_ref, wg_ref, wv_ref, *, tf):
    rows = w_ref.shape[0]
    for half, out_ref in ((0, wg_ref), (1, wv_ref)):
        for j in range(D_FF_PAD // tf):
            lo, hi = j * tf, min((j + 1) * tf, D_FF)
            blk = w_ref[:, half * D_FF + lo:half * D_FF + hi].astype(BF16)
            if hi - lo < tf:
                blk = jnp.concatenate([blk, jnp.zeros((rows, tf - (hi - lo)), BF16)], axis=1)
            out_ref[j] = blk


def _ffn_up_split(w_up, tf):
    tr = 128
    nf = D_FF_PAD // tf
    out = jax.ShapeDtypeStruct((nf, D_MODEL, tf), BF16)
    return pl.pallas_call(
        functools.partial(_ffn_up_split_body, tf=tf),
        grid=(D_MODEL // tr,),
        in_specs=[pl.BlockSpec((tr, 2 * D_FF), lambda i: (i, 0))],
        out_specs=[pl.BlockSpec((nf, tr, tf), lambda i: (0, i, 0))] * 2,
        out_shape=[out, out],
        compiler_params=_cparams("arbitrary"),
        name="ffn_up_split",
    )(w_up)


def _layer_weights(gla_w_gk, w_branch, w_out, ffn_w_up, ffn_conv, ffn_w_down, s5_w_glu, tf):
    w_gk_pad = jnp.concatenate(
        [gla_w_gk, jnp.zeros((LANES - GLA_RANK, GLA_K_WIDTH), F32)], axis=0).astype(BF16)
    pad = D_FF_PAD - D_FF
    padc = lambda a: jnp.pad(a, ((0, 0), (0, pad)))
    ffn_wg, ffn_wv = _ffn_up_split(ffn_w_up, tf)
    return dict(
        w_gk_pad=w_gk_pad,
        w_branch=w_branch.astype(BF16), w_out=w_out.astype(BF16),
        w_glu=s5_w_glu.astype(BF16),
        ffn_wg=ffn_wg, ffn_wv=ffn_wv,
        ffn_cg=padc(ffn_conv[:, :D_FF]), ffn_cv=padc(ffn_conv[:, D_FF:]),
        ffn_wd=jnp.pad(ffn_w_down, ((0, pad), (0, 0))).astype(BF16),
    )


def _layer(x, p, t):
    w = _layer_weights(p["gla_w_gk"], p["w_branch"], p["w_out"], p["ffn_w_up"],
                       p["ffn_conv"], p["ffn_w_down"], p["s5_w_glu"], t["tf_ffn"])
    row = lambda a: a.reshape(1, -1)
    z, zs, zst, h = _in_proj(x, row(p["norm_mix_pre"]), p["w_mix"], p["w_small"], t)
    y_a = _fox_attention_t(*_fox_prep(z, zs, p["fox_f_bias"], t), t)
    y_b = _gdn(z, zs, zst, p["gdn_conv"], p["gdn_a_log"], p["gdn_dt_bias"], p["gdn_norm"], t)
    mats = _s5_matrices(p["s5_a_re"], p["s5_a_im"], p["s5_b_re"], p["s5_b_im"], p["s5_c_re"],
                        p["s5_c_im"], p["s5_d"], p["s5_log_dt"])
    y_c = _s5(z, mats, w["w_glu"], t)
    y_d = _gla(z, zs, w["w_gk_pad"], p["gla_b_gk"], p["gla_norm"], t)
    merged = _merge(h, (y_a, y_b, y_c, y_d), p["w_gate"], w["w_branch"], t)
    x1, h2 = _out_proj(merged, w["w_out"], x, row(p["norm_mix_post"]), row(p["norm_ffn_pre"]), t)
    return _ffn(h2, x1, w["ffn_wg"], w["ffn_wv"], w["ffn_cg"], w["ffn_cv"], w["ffn_wd"],
                row(p["norm_ffn_post"]), t)


def kernel(x, norm_mix_pre, norm_mix_post, norm_ffn_pre, norm_ffn_post, w_in, fox_f_bias, gdn_conv, gdn_a_log, gdn_dt_bias, gdn_norm, s5_a_re, s5_a_im, s5_b_re, s5_b_im, s5_c_re, s5_c_im, s5_d, s5_log_dt, s5_w_glu, gla_w_gk, gla_b_gk, gla_norm, w_branch, w_out, ffn_w_up, ffn_conv, ffn_w_down):
    params = dict(
        norm_mix_pre=norm_mix_pre, norm_mix_post=norm_mix_post, norm_ffn_pre=norm_ffn_pre,
        norm_ffn_post=norm_ffn_post, w_in=w_in, fox_f_bias=fox_f_bias, gdn_conv=gdn_conv,
        gdn_a_log=gdn_a_log, gdn_dt_bias=gdn_dt_bias, gdn_norm=gdn_norm, s5_a_re=s5_a_re,
        s5_a_im=s5_a_im, s5_b_re=s5_b_re, s5_b_im=s5_b_im, s5_c_re=s5_c_re, s5_c_im=s5_c_im,
        s5_d=s5_d, s5_log_dt=s5_log_dt, s5_w_glu=s5_w_glu, gla_w_gk=gla_w_gk, gla_b_gk=gla_b_gk,
        gla_norm=gla_norm, w_branch=w_branch, w_out=w_out, ffn_w_up=ffn_w_up, ffn_conv=ffn_conv,
        ffn_w_down=ffn_w_down)
    batch, seq, _ = x.shape
    assert batch == 1
    t = _tiles(seq)
    xs = x[0]
    del params["w_in"]
    w_in_parts = _w_in_split(w_in)
    for layer in range(w_in.shape[0]):
        p = {k: v[layer] for k, v in params.items()}
        p["w_mix"], p["w_small"], p["w_gate"] = w_in_parts[layer]
        xs = _layer(xs, p, t)
    return xs[None]
```
